```python
import jax, jax.numpy as jnp
from jax import lax
import numpy as np

D_MODEL = 2048
BATCH = 4
SEQ = 4096
DEPTH = 1
DEC_BATCH = 16
DEC_SEQ = 32
PAST_LEN = 4096

CHUNK = 64
N_META = 16
D_CONF = 1024
D_SCONV = 1024
CONF_WIDTH = 31
SCONV_WIDTH = 3
FFN_WIDTH = 3
D_FF = 5632
EPS = 1e-6
IN_SIZES = (D_CONF, D_CONF, D_SCONV, D_SCONV, D_SCONV, D_MODEL, D_MODEL)
IN_SPLITS = tuple(int(s) for s in np.cumsum(IN_SIZES)[:-1])
D_IN = sum(IN_SIZES)

kernel_name = "hybrid_streaming_conv_encoder_step"


def rmsnorm(x, g):
    xf = x.astype(jnp.float32)
    y = xf * lax.rsqrt(jnp.mean(xf * xf, axis=-1, keepdims=True) + EPS)
    return (y * g.astype(jnp.float32)).astype(x.dtype)


def layernorm(x, g, b):
    xf = x.astype(jnp.float32)
    mu = jnp.mean(xf, axis=-1, keepdims=True)
    xc = xf - mu
    var = jnp.mean(xc * xc, axis=-1, keepdims=True)
    y = xc * lax.rsqrt(var + EPS) * g.astype(jnp.float32) + b.astype(jnp.float32)
    return y.astype(x.dtype)


def causal_dwconv(hist, u, w):
    width = w.shape[0]
    xp = jnp.concatenate([hist.astype(u.dtype), u], axis=1)
    y = lax.conv_general_dilated(
        xp, w[:, None, :].astype(u.dtype), window_strides=(1,), padding="VALID",
        dimension_numbers=("NWC", "WIO", "NWC"), feature_group_count=u.shape[-1])
    return y, xp[:, xp.shape[1] - (width - 1):]


def trunk_layer(x, hist_a, hist_b, hist_f, g_pre_mix, w_in, conf_conv_w, conf_conv_b,
                conf_ln_g, conf_ln_b, w_conf_out, sconv_w, w_sconv_out, w_o, g_post_mix,
                g_pre_ffn, w_up, ffn_conv_w, w_down, g_post_ffn):
    h = rmsnorm(x, g_pre_mix)
    z = jnp.einsum("bld,de->ble", h, w_in)
    a_val, a_gate, s_b, s_c, s_x, gate_a, gate_b = jnp.split(z, IN_SPLITS, axis=-1)
    a = a_val * jax.nn.sigmoid(a_gate)
    a_conv, new_a = causal_dwconv(hist_a, a, conf_conv_w)
    a_conv = a_conv + conf_conv_b
    a_out = jnp.einsum("blc,cd->bld", jax.nn.silu(layernorm(a_conv, conf_ln_g, conf_ln_b)), w_conf_out)
    u = s_c * s_x
    u_conv, new_b = causal_dwconv(hist_b, u, sconv_w)
    b_out = jnp.einsum("blc,cd->bld", s_b * u_conv, w_sconv_out)
    merged = jax.nn.sigmoid(gate_a) * a_out + jax.nn.sigmoid(gate_b) * b_out
    mix = jnp.einsum("bld,de->ble", merged, w_o)
    x = x + rmsnorm(mix, g_post_mix)
    h2 = rmsnorm(x, g_pre_ffn)
    up = jnp.einsum("bld,df->blf", h2, w_up)
    up_conv, new_f = causal_dwconv(hist_f, up, ffn_conv_w)
    f_gate, f_val = jnp.split(up_conv, 2, axis=-1)
    f = jnp.einsum("blf,fd->bld", jax.nn.silu(f_gate) * f_val, w_down)
    x = x + rmsnorm(f, g_post_ffn)
    return x, new_a, new_b, new_f


def setup_inputs(seed: int = 0) -> dict:
    key = jax.random.key(seed)
    ks = jax.random.split(key, 24)
    nrm = lambda k, shape, scale: jax.random.normal(k, shape, jnp.float32) * scale
    gain = lambda k, n: 1.0 + 0.05 * jax.random.normal(k, (DEPTH, n), jnp.float32)
    return {
        "x_prompt": nrm(ks[0], (BATCH, SEQ, D_MODEL), 1.0),
        "x_sample": nrm(ks[1], (DEC_BATCH, DEC_SEQ, D_MODEL), 1.0),
        "state_conf_conv": nrm(ks[2], (DEPTH, DEC_BATCH, CONF_WIDTH - 1, D_CONF), 1.0),
        "state_sconv": nrm(ks[3], (DEPTH, DEC_BATCH, SCONV_WIDTH - 1, D_SCONV), 1.0),
        "state_ffn_conv": nrm(ks[4], (DEPTH, DEC_BATCH, FFN_WIDTH - 1, 2 * D_FF), 1.0),
        "meta_tokens": nrm(ks[5], (N_META, D_MODEL), 1.0),
        "g_pre_mix": gain(ks[6], D_MODEL),
        "w_in": nrm(ks[7], (DEPTH, D_MODEL, D_IN), D_MODEL ** -0.5),
        "conf_conv_w": nrm(ks[8], (DEPTH, CONF_WIDTH, D_CONF), CONF_WIDTH ** -0.5),
        "conf_conv_b": nrm(ks[9], (DEPTH, D_CONF), 0.02),
        "conf_ln_g": gain(ks[10], D_CONF),
        "conf_ln_b": nrm(ks[11], (DEPTH, D_CONF), 0.02),
        "w_conf_out": nrm(ks[12], (DEPTH, D_CONF, D_MODEL), D_CONF ** -0.5),
        "sconv_w": nrm(ks[13], (DEPTH, SCONV_WIDTH, D_SCONV), SCONV_WIDTH ** -0.5),
        "w_sconv_out": nrm(ks[14], (DEPTH, D_SCONV, D_MODEL), D_SCONV ** -0.5),
        "w_o": nrm(ks[15], (DEPTH, D_MODEL, D_MODEL), D_MODEL ** -0.5),
        "g_post_mix": gain(ks[16], D_MODEL),
        "g_pre_ffn": gain(ks[17], D_MODEL),
        "w_up": nrm(ks[18], (DEPTH, D_MODEL, 2 * D_FF), D_MODEL ** -0.5),
        "ffn_conv_w": nrm(ks[19], (DEPTH, FFN_WIDTH, 2 * D_FF), FFN_WIDTH ** -0.5),
        "w_down": nrm(ks[20], (DEPTH, D_FF, D_MODEL), D_FF ** -0.5),
        "g_post_ffn": gain(ks[21], D_MODEL),
    }


def reference(x_prompt, x_sample, state_conf_conv, state_sconv, state_ffn_conv, meta_tokens,
              g_pre_mix, w_in, conf_conv_w, conf_conv_b, conf_ln_g, conf_ln_b, w_conf_out,
              sconv_w, w_sconv_out, w_o, g_post_mix, g_pre_ffn, w_up, ffn_conv_w, w_down,
              g_post_ffn):
    dt = x_prompt.dtype
    meta = jnp.broadcast_to(meta_tokens.astype(dt)[None], (x_prompt.shape[0], N_META, D_MODEL))
    xp = jnp.concatenate([meta, x_prompt], axis=1)
    xs = x_sample
    pa, pb, pf, sa, sb, sf = [], [], [], [], [], []
    for l in range(DEPTH):
        weights = (g_pre_mix[l], w_in[l], conf_conv_w[l], conf_conv_b[l], conf_ln_g[l],
                   conf_ln_b[l], w_conf_out[l], sconv_w[l], w_sconv_out[l], w_o[l],
                   g_post_mix[l], g_pre_ffn[l], w_up[l], ffn_conv_w[l], w_down[l], g_post_ffn[l])
        bp = xp.shape[0]
        xp, na, nb, nf = trunk_layer(
            xp, jnp.zeros((bp, CONF_WIDTH - 1, D_CONF), dt), jnp.zeros((bp, SCONV_WIDTH - 1, D_SCONV), dt),
            jnp.zeros((bp, FFN_WIDTH - 1, 2 * D_FF), dt), *weights)
        pa.append(na); pb.append(nb); pf.append(nf)
        xs, ma, mb, mf = trunk_layer(xs, state_conf_conv[l], state_sconv[l], state_ffn_conv[l], *weights)
        sa.append(ma); sb.append(mb); sf.append(mf)
    y_prompt = xp[:, N_META:]
    y_sample = xs
    return (y_prompt, y_sample, jnp.stack(pa), jnp.stack(pb), jnp.stack(pf),
            jnp.stack(sa), jnp.stack(sb), jnp.stack(sf))
```

```python
import functools

import jax
import jax.numpy as jnp
from jax import lax
from jax.experimental import pallas as pl
from jax.experimental.pallas import tpu as pltpu

EPS = 1e-6
N_SPLIT_IN = 9
SUBLANES = 8
CONV_ROW_CHUNK = 32
V7X_VMEM_BYTES = 64 * 1024 * 1024
VMEM_LIMIT_BYTES = V7X_VMEM_BYTES - 8 * 1024 * 1024

F32 = jnp.float32
BF16 = jnp.bfloat16


def _round_up(n, m):
    return (n + m - 1) // m * m


def _rmsnorm_rows(xf, g):
    ms = jnp.mean(xf * xf, axis=-1, keepdims=True)
    return xf * lax.rsqrt(ms + EPS) * g


def _dot(a, b):
    return jnp.dot(a, b, preferred_element_type=F32)


def _mix_in_kernel(x_ref, hista_ref, histb_ref, gpre_ref, wv_ref, wg_ref, wb_ref, wc_ref, wx_ref,
                   cw_ref, cb_ref, lng_ref, lnb_ref, sw_ref,
                   acta_ref, actb_ref, newa_ref, newb_ref,
                   h_scr, xpa_scr, xpu_scr, aconv_scr, *, sb, lt, nc, tc, wa, wu):
    t = pl.program_id(1)
    c = pl.program_id(2)
    m = sb * lt
    d_model = x_ref.shape[-1]
    pad_a = xpa_scr.shape[2] - lt
    pad_u = xpu_scr.shape[2] - lt

    @pl.when(c == 0)
    def _():
        xf = x_ref[...].reshape(m, d_model)
        h_scr[...] = _rmsnorm_rows(xf, gpre_ref[...]).astype(BF16)

    @pl.when((t == 0) & (c == 0))
    def _():
        for cc in range(nc):
            xpa_scr[cc, :, pad_a - (wa - 1):pad_a, :] = hista_ref[:, :, cc * tc:(cc + 1) * tc]
            xpu_scr[cc, :, pad_u - (wu - 1):pad_u, :] = histb_ref[:, :, cc * tc:(cc + 1) * tc]

    h = h_scr[...]
    a = _dot(h, wv_ref[...]) * jax.nn.sigmoid(_dot(h, wg_ref[...]))
    xpa_scr[c, :, pad_a:pad_a + lt, :] = a.reshape(sb, lt, tc)
    rc = min(lt, CONV_ROW_CHUNK)
    for s in range(sb):
        for r0 in range(0, lt, rc):
            acc = None
            for k in range(wa):
                lo = pad_a - (wa - 1) + k + r0
                term = cw_ref[k:k + 1, :] * xpa_scr[c, s, lo:lo + rc, :]
                acc = term if acc is None else acc + term
            aconv_scr[c, s, r0:r0 + rc, :] = acc + cb_ref[...]
    xpa_scr[c, :, pad_a - (wa - 1):pad_a, :] = xpa_scr[c, :, pad_a + lt - (wa - 1):pad_a + lt, :]

    u = _dot(h, wc_ref[...]) * _dot(h, wx_ref[...])
    xpu_scr[c, :, pad_u:pad_u + lt, :] = u.reshape(sb, lt, tc)
    uconv = None
    for k in range(wu):
        lo = pad_u - (wu - 1) + k
        term = sw_ref[k:k + 1, :] * xpu_scr[c, :, lo:lo + lt, :]
        uconv = term if uconv is None else uconv + term
    s_b = _dot(h, wb_ref[...]).reshape(sb, lt, tc)
    actb_ref[...] = (s_b * uconv).astype(BF16)
    xpu_scr[c, :, pad_u - (wu - 1):pad_u, :] = xpu_scr[c, :, pad_u + lt - (wu - 1):pad_u + lt, :]

    @pl.when(c == nc - 1)
    def _():
        d_conf = nc * tc
        tot = None
        for cc in range(nc):
            part = jnp.sum(aconv_scr[cc], axis=-1, keepdims=True)
            tot = part if tot is None else tot + part
        mu = tot / d_conf
        sq = None
        for cc in range(nc):
            xc = aconv_scr[cc] - mu
            part = jnp.sum(xc * xc, axis=-1, keepdims=True)
            sq = part if sq is None else sq + part
        rstd = lax.rsqrt(sq / d_conf + EPS)
        for cc in range(nc):
            sl = slice(cc * tc, (cc + 1) * tc)
            y = (aconv_scr[cc] - mu) * rstd * lng_ref[:, sl] + lnb_ref[:, sl]
            acta_ref[:, :, sl] = (y * jax.nn.sigmoid(y)).astype(BF16)
            newa_ref[:, :, sl] = xpa_scr[cc, :, pad_a - (wa - 1):pad_a, :]
            newb_ref[:, :, sl] = xpu_scr[cc, :, pad_u - (wu - 1):pad_u, :]


def _mix_in(x, hist_a, hist_b, w, *, sb, lt, tc):
    ns, seq, d_model = x.shape
    d_conf = hist_a.shape[-1]
    wa = hist_a.shape[1] + 1
    wu = hist_b.shape[1] + 1
    nc = d_conf // tc
    nt = seq // lt
    m = sb * lt
    pad_a = _round_up(wa - 1, SUBLANES)
    pad_u = _round_up(wu - 1, SUBLANES)
    grid = (ns // sb, nt, nc)

    def w_in_spec(group):
        return pl.BlockSpec((d_model, tc), lambda s, t, c, g=group: (0, g * nc + c))

    row3 = lambda s, t, c: (s, t, 0)
    stream3 = lambda s, t, c: (s, 0, 0)
    chan2 = lambda s, t, c: (0, c)
    fixed2 = lambda s, t, c: (0, 0)
    kern = functools.partial(_mix_in_kernel, sb=sb, lt=lt, nc=nc, tc=tc, wa=wa, wu=wu)
    return pl.pallas_call(
        kern,
        grid=grid,
        in_specs=[
            pl.BlockSpec((sb, lt, d_model), row3),
            pl.BlockSpec((sb, wa - 1, d_conf), stream3),
            pl.BlockSpec((sb, wu - 1, d_conf), stream3),
            pl.BlockSpec((1, d_model), fixed2),
            w_in_spec(0), w_in_spec(1), w_in_spec(2), w_in_spec(3), w_in_spec(4),
            pl.BlockSpec((wa, tc), chan2),
            pl.BlockSpec((1, tc), chan2),
            pl.BlockSpec((1, d_conf), fixed2),
            pl.BlockSpec((1, d_conf), fixed2),
            pl.BlockSpec((wu, tc), chan2),
        ],
        out_specs=[
            pl.BlockSpec((sb, lt, d_conf), row3),
            pl.BlockSpec((sb, lt, tc), lambda s, t, c: (s, t, c)),
            pl.BlockSpec((sb, wa - 1, d_conf), stream3),
            pl.BlockSpec((sb, wu - 1, d_conf), stream3),
        ],
        out_shape=[
            jax.ShapeDtypeStruct((ns, seq, d_conf), BF16),
            jax.ShapeDtypeStruct((ns, seq, d_conf), BF16),
            jax.ShapeDtypeStruct(hist_a.shape, F32),
            jax.ShapeDtypeStruct(hist_b.shape, F32),
        ],
        scratch_shapes=[
            pltpu.VMEM((m, d_model), BF16),
            pltpu.VMEM((nc, sb, pad_a + lt, tc), F32),
            pltpu.VMEM((nc, sb, pad_u + lt, tc), F32),
            pltpu.VMEM((nc, sb, lt, tc), F32),
        ],
        compiler_params=pltpu.CompilerParams(
            dimension_semantics=("arbitrary", "arbitrary", "arbitrary"),
            vmem_limit_bytes=VMEM_LIMIT_BYTES),
        name="mix_in",
    )(x, hist_a, hist_b, w["g_pre_mix"], w["w_in"], w["w_in"], w["w_in"], w["w_in"], w["w_in"],
      w["conf_conv_w"], w["conf_conv_b"], w["conf_ln_g"], w["conf_ln_b"], w["sconv_w"])


def _mix_out_kernel(x_ref, acta_ref, actb_ref, gpre_ref, wco_ref, wso_ref, wga_ref, wgb_ref,
                    wo_ref, gpost_ref, out_ref, h_scr, acc_scr, *, m, nn):
    n = pl.program_id(2)
    d_model = x_ref.shape[-1]

    @pl.when(n == 0)
    def _():
        xf = x_ref[...].reshape(m, d_model)
        h_scr[...] = _rmsnorm_rows(xf, gpre_ref[...]).astype(BF16)
        acc_scr[...] = jnp.zeros_like(acc_scr)

    h = h_scr[...]
    a_out = _dot(acta_ref[...].reshape(m, -1), wco_ref[...])
    b_out = _dot(actb_ref[...].reshape(m, -1), wso_ref[...])
    gate_a = jax.nn.sigmoid(_dot(h, wga_ref[...]))
    gate_b = jax.nn.sigmoid(_dot(h, wgb_ref[...]))
    merged = (gate_a * a_out + gate_b * b_out).astype(BF16)
    acc_scr[...] += _dot(merged, wo_ref[...])

    @pl.when(n == nn - 1)
    def _():
        xf = x_ref[...].reshape(m, d_model)
        out_ref[...] = (xf + _rmsnorm_rows(acc_scr[...], gpost_ref[...])).reshape(out_ref.shape)


def _mix_out(x, act_a, act_b, w, *, sb, lt, tn):
    ns, seq, d_model = x.shape
    d_conf = act_a.shape[-1]
    nn = d_model // tn
    nt = seq // lt
    m = sb * lt
    gate_a_blk = 5 * d_conf // tn
    gate_b_blk = gate_a_blk + nn
    row3 = lambda s, t, n: (s, t, 0)
    fixed2 = lambda s, t, n: (0, 0)
    col2 = lambda s, t, n: (0, n)
    kern = functools.partial(_mix_out_kernel, m=m, nn=nn)
    return pl.pallas_call(
        kern,
        grid=(ns // sb, nt, nn),
        in_specs=[
            pl.BlockSpec((sb, lt, d_model), row3),
            pl.BlockSpec((sb, lt, d_conf), row3),
            pl.BlockSpec((sb, lt, d_conf), row3),
            pl.BlockSpec((1, d_model), fixed2),
            pl.BlockSpec((d_conf, tn), col2),
            pl.BlockSpec((d_conf, tn), col2),
            pl.BlockSpec((d_model, tn), lambda s, t, n: (0, gate_a_blk + n)),
            pl.BlockSpec((d_model, tn), lambda s, t, n: (0, gate_b_blk + n)),
            pl.BlockSpec((tn, d_model), lambda s, t, n: (n, 0)),
            pl.BlockSpec((1, d_model), fixed2),
        ],
        out_specs=pl.BlockSpec((sb, lt, d_model), row3),
        out_shape=jax.ShapeDtypeStruct(x.shape, F32),
        scratch_shapes=[
            pltpu.VMEM((m, d_model), BF16),
            pltpu.VMEM((m, d_model), F32),
        ],
        compiler_params=pltpu.CompilerParams(
            dimension_semantics=("arbitrary", "arbitrary", "arbitrary"),
            vmem_limit_bytes=VMEM_LIMIT_BYTES),
        name="mix_out",
    )(x, act_a, act_b, w["g_pre_mix"], w["w_conf_out"], w["w_sconv_out"], w["w_in"], w["w_in"],
      w["w_o"], w["g_post_mix"])


def _ffn_kernel(x_ref, histg_ref, histv_ref, gpre_ref, wug_ref, wuv_ref, cwg_ref, cwv_ref,
                wd_ref, gpost_ref, out_ref, newg_ref, newv_ref,
                h_scr, acc_scr, carry_scr, xp_scr, *, sb, lt, nj, wf):
    t = pl.program_id(1)
    j = pl.program_id(2)
    m = sb * lt
    d_model = x_ref.shape[-1]
    tj = wug_ref.shape[-1]
    pad = xp_scr.shape[1] - lt

    @pl.when(j == 0)
    def _():
        xf = x_ref[...].reshape(m, d_model)
        h_scr[...] = _rmsnorm_rows(xf, gpre_ref[...]).astype(BF16)
        acc_scr[...] = jnp.zeros_like(acc_scr)

    h = h_scr[...]

    def conv(up, hist_ref, cw_ref, new_ref, slot):
        @pl.when(t == 0)
        def _():
            xp_scr[:, pad - (wf - 1):pad, :] = hist_ref[...]

        @pl.when(t != 0)
        def _():
            xp_scr[:, 0:pad, :] = carry_scr[slot]

        xp_scr[:, pad:pad + lt, :] = up.reshape(sb, lt, tj)
        y = None
        for k in range(wf):
            lo = pad - (wf - 1) + k
            term = cw_ref[k:k + 1, :] * xp_scr[:, lo:lo + lt, :]
            y = term if y is None else y + term
        carry_scr[slot] = xp_scr[:, lt:lt + pad, :]
        new_ref[...] = xp_scr[:, pad + lt - (wf - 1):pad + lt, :]
        return y.reshape(m, tj)

    f_gate = conv(_dot(h, wug_ref[...]), histg_ref, cwg_ref, newg_ref, j)
    f_val = conv(_dot(h, wuv_ref[...]), histv_ref, cwv_ref, newv_ref, nj + j)
    act = (f_gate * jax.nn.sigmoid(f_gate) * f_val).astype(BF16)
    acc_scr[...] += _dot(act, wd_ref[...])

    @pl.when(j == nj - 1)
    def _():
        xf = x_ref[...].reshape(m, d_model)
        out_ref[...] = (xf + _rmsnorm_rows(acc_scr[...], gpost_ref[...])).reshape(out_ref.shape)


def _ffn(x, hist_f, w, *, sb, lt, tj):
    ns, seq, d_model = x.shape
    d_ff = hist_f.shape[-1] // 2
    wf = hist_f.shape[1] + 1
    nj = d_ff // tj
    nt = seq // lt
    m = sb * lt
    pad = _round_up(wf - 1, SUBLANES)
    assert lt >= pad
    row3 = lambda s, t, j: (s, t, 0)
    fixed2 = lambda s, t, j: (0, 0)
    gate2 = lambda s, t, j: (0, j)
    val2 = lambda s, t, j: (0, nj + j)
    gate3 = lambda s, t, j: (s, 0, j)
    val3 = lambda s, t, j: (s, 0, nj + j)
    kern = functools.partial(_ffn_kernel, sb=sb, lt=lt, nj=nj, wf=wf)
    y, new_g, new_v = pl.pallas_call(
        kern,
        grid=(ns // sb, nt, nj),
        in_specs=[
            pl.BlockSpec((sb, lt, d_model), row3),
            pl.BlockSpec((sb, wf - 1, tj), gate3),
            pl.BlockSpec((sb, wf - 1, tj), val3),
            pl.BlockSpec((1, d_model), fixed2),
            pl.BlockSpec((d_model, tj), gate2),
            pl.BlockSpec((d_model, tj), val2),
            pl.BlockSpec((wf, tj), gate2),
            pl.BlockSpec((wf, tj), val2),
            pl.BlockSpec((tj, d_model), lambda s, t, j: (j, 0)),
            pl.BlockSpec((1, d_model), fixed2),
        ],
        out_specs=[
            pl.BlockSpec((sb, lt, d_model), row3),
            pl.BlockSpec((sb, wf - 1, tj), gate3),
            pl.BlockSpec((sb, wf - 1, tj), gate3),
        ],
        out_shape=[
            jax.ShapeDtypeStruct(x.shape, F32),
            jax.ShapeDtypeStruct((ns, wf - 1, d_ff), F32),
            jax.ShapeDtypeStruct((ns, wf - 1, d_ff), F32),
        ],
        scratch_shapes=[
            pltpu.VMEM((m, d_model), BF16),
            pltpu.VMEM((m, d_model), F32),
            pltpu.VMEM((2 * nj, sb, pad, tj), F32),
            pltpu.VMEM((sb, pad + lt, tj), F32),
        ],
        compiler_params=pltpu.CompilerParams(
            dimension_semantics=("arbitrary", "arbitrary", "arbitrary"),
            vmem_limit_bytes=VMEM_LIMIT_BYTES),
        name="ffn",
    )(x, hist_f, hist_f, w["g_pre_ffn"], w["w_up"], w["w_up"], w["ffn_conv_w"], w["ffn_conv_w"],
      w["w_down"], w["g_post_ffn"])
    return y, jnp.concatenate([new_g, new_v], axis=-1)


def _layer(x, hist_a, hist_b, hist_f, w, *, sb, lt):
    act_a, act_b, new_a, new_b = _mix_in(x, hist_a, hist_b, w, sb=sb, lt=lt, tc=256)
    x1 = _mix_out(x, act_a, act_b, w, sb=sb, lt=lt, tn=256)
    y, new_f = _ffn(x1, hist_f, w, sb=sb, lt=lt, tj=512)
    return y, new_a, new_b, new_f


def kernel(x_prompt, x_sample, state_conf_conv, state_sconv, state_ffn_conv, meta_tokens, g_pre_mix, w_in, conf_conv_w, conf_conv_b, conf_ln_g, conf_ln_b, w_conf_out, sconv_w, w_sconv_out, w_o, g_post_mix, g_pre_ffn, w_up, ffn_conv_w, w_down, g_post_ffn):
    depth = w_in.shape[0]
    batch, seq, _ = x_prompt.shape
    dec_batch, dec_seq, _ = x_sample.shape
    n_meta = meta_tokens.shape[0]
    prompt_tile = 512
    assert seq % prompt_tile == 0

    xm = meta_tokens.astype(x_prompt.dtype)[None]
    xp = x_prompt
    xs = x_sample
    outs = [[] for _ in range(6)]
    for l in range(depth):
        w = {
            "g_pre_mix": g_pre_mix[l][None], "w_in": w_in[l].astype(BF16),
            "conf_conv_w": conf_conv_w[l], "conf_conv_b": conf_conv_b[l][None],
            "conf_ln_g": conf_ln_g[l][None], "conf_ln_b": conf_ln_b[l][None],
            "w_conf_out": w_conf_out[l].astype(BF16), "sconv_w": sconv_w[l],
            "w_sconv_out": w_sconv_out[l].astype(BF16), "w_o": w_o[l].astype(BF16),
            "g_post_mix": g_post_mix[l][None], "g_pre_ffn": g_pre_ffn[l][None],
            "w_up": w_up[l].astype(BF16), "ffn_conv_w": ffn_conv_w[l],
            "w_down": w_down[l].astype(BF16), "g_post_ffn": g_post_ffn[l][None],
        }
        zeros = lambda ref: jnp.zeros((1,) + ref.shape[2:], x_prompt.dtype)
        xm, ma, mb, mf = _layer(xm, zeros(state_conf_conv), zeros(state_sconv), zeros(state_ffn_conv),
                                w, sb=1, lt=n_meta)
        bcast = lambda hist: jnp.broadcast_to(hist, (batch,) + hist.shape[1:])
        xp, pa, pb, pf = _layer(xp, bcast(ma), bcast(mb), bcast(mf), w, sb=1, lt=prompt_tile)
        xs, sa, sb_, sf = _layer(xs, state_conf_conv[l], state_sconv[l], state_ffn_conv[l], w,
                                 sb=dec_batch, lt=dec_seq)
        for acc, val in zip(outs, (pa, pb, pf, sa, sb_, sf)):
            acc.append(val)
    return (xp, xs) + tuple(jnp.stack(o) for o in outs)
```

```python
import functools

import jax
import jax.numpy as jnp
from jax import lax
from jax.experimental import pallas as pl
from jax.experimental.pallas import tpu as pltpu

EPS = 1e-6
N_SPLIT_IN = 9
SUBLANES = 8
CONV_ROW_CHUNK = 32
MIX_IN_CHUNK_ROWS = 256
V7X_VMEM_BYTES = 64 * 1024 * 1024
VMEM_LIMIT_BYTES = V7X_VMEM_BYTES - 8 * 1024 * 1024

F32 = jnp.float32
BF16 = jnp.bfloat16


def _round_up(n, m):
    return (n + m - 1) // m * m


def _rmsnorm_rows(xf, g):
    ms = jnp.mean(xf * xf, axis=-1, keepdims=True)
    return xf * lax.rsqrt(ms + EPS) * g


def _dot(a, b):
    return jnp.dot(a, b, preferred_element_type=F32)


def _row_chunks(sb, lt, rows):
    if lt >= rows:
        assert lt % rows == 0
        return [(s, s + 1, r0, r0 + rows) for s in range(sb) for r0 in range(0, lt, rows)]
    per = max(1, rows // lt)
    return [(s0, min(s0 + per, sb), 0, lt) for s0 in range(0, sb, per)]


def _mix_in_kernel(x_ref, hista_ref, histb_ref, gpre_ref, wv_ref, wg_ref, wb_ref, wc_ref, wx_ref,
                   cw_ref, cb_ref, lng_ref, lnb_ref, sw_ref,
                   acta_ref, actb_ref, newa_ref, newb_ref,
                   h_scr, xpa_scr, xpu_scr, aconv_scr, ph_scr, *, sb, lt, nc, tc, wa, wu):
    t = pl.program_id(1)
    c = pl.program_id(2)
    m = sb * lt
    d_model = x_ref.shape[-1]
    pad_a = xpa_scr.shape[2] - lt
    pad_u = xpu_scr.shape[2] - lt

    @pl.when(c == 0)
    def _():
        xf = x_ref[...].reshape(m, d_model)
        h_scr[...] = _rmsnorm_rows(xf, gpre_ref[...]).astype(BF16)

    @pl.when((t == 0) & (c == 0))
    def _():
        for cc in range(nc):
            xpa_scr[cc, :, pad_a - (wa - 1):pad_a, :] = hista_ref[:, :, cc * tc:(cc + 1) * tc]
            xpu_scr[cc, :, pad_u - (wu - 1):pad_u, :] = histb_ref[:, :, cc * tc:(cc + 1) * tc]

    rc = min(lt, CONV_ROW_CHUNK)
    for s0, s1, r0, r1 in _row_chunks(sb, lt, MIX_IN_CHUNK_ROWS):
        nr = r1 - r0
        row0 = s0 * lt + r0
        hc = h_scr[row0:row0 + (s1 - s0) * nr, :]
        shape3 = (s1 - s0, nr, tc)
        a = _dot(hc, wv_ref[...]) * jax.nn.sigmoid(_dot(hc, wg_ref[...]))
        xpa_scr[c, s0:s1, pad_a + r0:pad_a + r1, :] = a.reshape(shape3)
        lo = 0 if r0 == 0 else r0 + pad_a - SUBLANES
        hi = r1 + pad_a - SUBLANES
        for p in range(1, SUBLANES):
            ph_scr[p - 1, s0:s1, lo:hi, :] = xpa_scr[c, s0:s1, lo + p:hi + p, :]
        for s in range(s0, s1):
            for q0 in range(r0, r1, rc):
                acc = None
                for k in range(wa):
                    q, p = divmod(pad_a - (wa - 1) + k, SUBLANES)
                    at = q * SUBLANES + q0
                    if p == 0:
                        src = xpa_scr[c, s, at:at + rc, :]
                    else:
                        src = ph_scr[p - 1, s, at:at + rc, :]
                    term = cw_ref[k:k + 1, :] * src
                    acc = term if acc is None else acc + term
                aconv_scr[c, s, q0:q0 + rc, :] = acc + cb_ref[...]
        u = _dot(hc, wc_ref[...]) * _dot(hc, wx_ref[...])
        xpu_scr[c, s0:s1, pad_u + r0:pad_u + r1, :] = u.reshape(shape3)
        uconv = None
        for k in range(wu):
            at = pad_u - (wu - 1) + k + r0
            term = sw_ref[k:k + 1, :] * xpu_scr[c, s0:s1, at:at + nr, :]
            uconv = term if uconv is None else uconv + term
        s_b = _dot(hc, wb_ref[...]).reshape(shape3)
        actb_ref[s0:s1, r0:r1, :] = (s_b * uconv).astype(BF16)
    xpa_scr[c, :, pad_a - (wa - 1):pad_a, :] = xpa_scr[c, :, pad_a + lt - (wa - 1):pad_a + lt, :]
    xpu_scr[c, :, pad_u - (wu - 1):pad_u, :] = xpu_scr[c, :, pad_u + lt - (wu - 1):pad_u + lt, :]

    @pl.when(c == nc - 1)
    def _():
        d_conf = nc * tc
        tot = None
        for cc in range(nc):
            part = jnp.sum(aconv_scr[cc], axis=-1, keepdims=True)
            tot = part if tot is None else tot + part
        mu = tot / d_conf
        sq = None
        for cc in range(nc):
            xc = aconv_scr[cc] - mu
            part = jnp.sum(xc * xc, axis=-1, keepdims=True)
            sq = part if sq is None else sq + part
        rstd = lax.rsqrt(sq / d_conf + EPS)
        for cc in range(nc):
            sl = slice(cc * tc, (cc + 1) * tc)
            y = (aconv_scr[cc] - mu) * rstd * lng_ref[:, sl] + lnb_ref[:, sl]
            acta_ref[:, :, sl] = (y * jax.nn.sigmoid(y)).astype(BF16)
            newa_ref[:, :, sl] = xpa_scr[cc, :, pad_a - (wa - 1):pad_a, :]
            newb_ref[:, :, sl] = xpu_scr[cc, :, pad_u - (wu - 1):pad_u, :]


def _mix_in(x, hist_a, hist_b, w, *, sb, lt, tc):
    ns, seq, d_model = x.shape
    d_conf = hist_a.shape[-1]
    wa = hist_a.shape[1] + 1
    wu = hist_b.shape[1] + 1
    nc = d_conf // tc
    nt = seq // lt
    m = sb * lt
    pad_a = _round_up(wa - 1, SUBLANES)
    pad_u = _round_up(wu - 1, SUBLANES)
    grid = (ns // sb, nt, nc)

    def w_in_spec(group):
        return pl.BlockSpec((d_model, tc), lambda s, t, c, g=group: (0, g * nc + c))

    row3 = lambda s, t, c: (s, t, 0)
    stream3 = lambda s, t, c: (s, 0, 0)
    chan2 = lambda s, t, c: (0, c)
    fixed2 = lambda s, t, c: (0, 0)
    kern = functools.partial(_mix_in_kernel, sb=sb, lt=lt, nc=nc, tc=tc, wa=wa, wu=wu)
    return pl.pallas_call(
        kern,
        grid=grid,
        in_specs=[
            pl.BlockSpec((sb, lt, d_model), row3),
            pl.BlockSpec((sb, wa - 1, d_conf), stream3),
            pl.BlockSpec((sb, wu - 1, d_conf), stream3),
            pl.BlockSpec((1, d_model), fixed2),
            w_in_spec(0), w_in_spec(1), w_in_spec(2), w_in_spec(3), w_in_spec(4),
            pl.BlockSpec((wa, tc), chan2),
            pl.BlockSpec((1, tc), chan2),
            pl.BlockSpec((1, d_conf), fixed2),
            pl.BlockSpec((1, d_conf), fixed2),
            pl.BlockSpec((wu, tc), chan2),
        ],
        out_specs=[
            pl.BlockSpec((sb, lt, d_conf), row3),
            pl.BlockSpec((sb, lt, tc), lambda s, t, c: (s, t, c)),
            pl.BlockSpec((sb, wa - 1, d_conf), stream3),
            pl.BlockSpec((sb, wu - 1, d_conf), stream3),
        ],
        out_shape=[
            jax.ShapeDtypeStruct((ns, seq, d_conf), BF16),
            jax.ShapeDtypeStruct((ns, seq, d_conf), BF16),
            jax.ShapeDtypeStruct(hist_a.shape, F32),
            jax.ShapeDtypeStruct(hist_b.shape, F32),
        ],
        scratch_shapes=[
            pltpu.VMEM((m, d_model), BF16),
            pltpu.VMEM((nc, sb, pad_a + lt, tc), F32),
            pltpu.VMEM((nc, sb, pad_u + lt, tc), F32),
            pltpu.VMEM((nc, sb, lt, tc), F32),
            pltpu.VMEM((SUBLANES - 1, sb, lt + pad_a - SUBLANES, tc), F32),
        ],
        compiler_params=pltpu.CompilerParams(
            dimension_semantics=("arbitrary", "arbitrary", "arbitrary"),
            vmem_limit_bytes=VMEM_LIMIT_BYTES),
        name="mix_in",
    )(x, hist_a, hist_b, w["g_pre_mix"], w["w_in"], w["w_in"], w["w_in"], w["w_in"], w["w_in"],
      w["conf_conv_w"], w["conf_conv_b"], w["conf_ln_g"], w["conf_ln_b"], w["sconv_w"])


def _mix_out_kernel(x_ref, acta_ref, actb_ref, gpre_ref, wco_ref, wso_ref, wga_ref, wgb_ref,
                    wo_ref, gpost_ref, out_ref, h_scr, acc_scr, *, m, nn):
    n = pl.program_id(2)
    d_model = x_ref.shape[-1]

    @pl.when(n == 0)
    def _():
        xf = x_ref[...].reshape(m, d_model)
        h_scr[...] = _rmsnorm_rows(xf, gpre_ref[...]).astype(BF16)
        acc_scr[...] = jnp.zeros_like(acc_scr)

    h = h_scr[...]
    a_out = _dot(acta_ref[...].reshape(m, -1), wco_ref[...])
    b_out = _dot(actb_ref[...].reshape(m, -1), wso_ref[...])
    gate_a = jax.nn.sigmoid(_dot(h, wga_ref[...]))
    gate_b = jax.nn.sigmoid(_dot(h, wgb_ref[...]))
    merged = (gate_a * a_out + gate_b * b_out).astype(BF16)
    acc_scr[...] += _dot(merged, wo_ref[...])

    @pl.when(n == nn - 1)
    def _():
        xf = x_ref[...].reshape(m, d_model)
        out_ref[...] = (xf + _rmsnorm_rows(acc_scr[...], gpost_ref[...])).reshape(out_ref.shape)


def _mix_out(x, act_a, act_b, w, *, sb, lt, tn):
    ns, seq, d_model = x.shape
    d_conf = act_a.shape[-1]
    nn = d_model // tn
    nt = seq // lt
    m = sb * lt
    gate_a_blk = 5 * d_conf // tn
    gate_b_blk = gate_a_blk + nn
    row3 = lambda s, t, n: (s, t, 0)
    fixed2 = lambda s, t, n: (0, 0)
    col2 = lambda s, t, n: (0, n)
    kern = functools.partial(_mix_out_kernel, m=m, nn=nn)
    return pl.pallas_call(
        kern,
        grid=(ns // sb, nt, nn),
        in_specs=[
            pl.BlockSpec((sb, lt, d_model), row3),
            pl.BlockSpec((sb, lt, d_conf), row3),
            pl.BlockSpec((sb, lt, d_conf), row3),
            pl.BlockSpec((1, d_model), fixed2),
            pl.BlockSpec((d_conf, tn), col2),
            pl.BlockSpec((d_conf, tn), col2),
            pl.BlockSpec((d_model, tn), lambda s, t, n: (0, gate_a_blk + n)),
            pl.BlockSpec((d_model, tn), lambda s, t, n: (0, gate_b_blk + n)),
            pl.BlockSpec((tn, d_model), lambda s, t, n: (n, 0)),
            pl.BlockSpec((1, d_model), fixed2),
        ],
        out_specs=pl.BlockSpec((sb, lt, d_model), row3),
        out_shape=jax.ShapeDtypeStruct(x.shape, F32),
        scratch_shapes=[
            pltpu.VMEM((m, d_model), BF16),
            pltpu.VMEM((m, d_model), F32),
        ],
        compiler_params=pltpu.CompilerParams(
            dimension_semantics=("arbitrary", "arbitrary", "arbitrary"),
            vmem_limit_bytes=VMEM_LIMIT_BYTES),
        name="mix_out",
    )(x, act_a, act_b, w["g_pre_mix"], w["w_conf_out"], w["w_sconv_out"], w["w_in"], w["w_in"],
      w["w_o"], w["g_post_mix"])


def _ffn_kernel(x_ref, histg_ref, histv_ref, gpre_ref, wug_ref, wuv_ref, cwg_ref, cwv_ref,
                wd_ref, gpost_ref, out_ref, newg_ref, newv_ref,
                h_scr, acc_scr, carry_scr, xpg_scr, xpv_scr, *, sb, lt, nj, wf):
    t = pl.program_id(1)
    j = pl.program_id(2)
    m = sb * lt
    d_model = x_ref.shape[-1]
    tj = wug_ref.shape[-1]
    pad = xpg_scr.shape[1] - lt

    @pl.when(j == 0)
    def _():
        xf = x_ref[...].reshape(m, d_model)
        h_scr[...] = _rmsnorm_rows(xf, gpre_ref[...]).astype(BF16)
        acc_scr[...] = jnp.zeros_like(acc_scr)

    @pl.when(t == 0)
    def _():
        for slot, hist_ref in ((j, histg_ref), (nj + j, histv_ref)):
            carry_scr[slot] = jnp.zeros(carry_scr.shape[1:], F32)
            carry_scr[slot, :, pad - (wf - 1):pad, :] = hist_ref[...]

    h = h_scr[...]

    def conv(up, cw_ref, new_ref, slot, xp_scr):
        xp_scr[:, 0:pad, :] = carry_scr[slot]
        xp_scr[:, pad:pad + lt, :] = up.reshape(sb, lt, tj)
        y = None
        for k in range(wf):
            lo = pad - (wf - 1) + k
            term = cw_ref[k:k + 1, :] * xp_scr[:, lo:lo + lt, :]
            y = term if y is None else y + term
        carry_scr[slot] = xp_scr[:, lt:lt + pad, :]
        new_ref[...] = xp_scr[:, pad + lt - (wf - 1):pad + lt, :]
        return y.reshape(m, tj)

    f_gate = conv(_dot(h, wug_ref[...]), cwg_ref, newg_ref, j, xpg_scr)
    f_val = conv(_dot(h, wuv_ref[...]), cwv_ref, newv_ref, nj + j, xpv_scr)
    act = (f_gate * jax.nn.sigmoid(f_gate) * f_val).astype(BF16)
    acc_scr[...] += _dot(act, wd_ref[...])

    @pl.when(j == nj - 1)
    def _():
        xf = x_ref[...].reshape(m, d_model)
        out_ref[...] = (xf + _rmsnorm_rows(acc_scr[...], gpost_ref[...])).reshape(out_ref.shape)


def _ffn(x, hist_f, w, *, sb, lt, tj):
    ns, seq, d_model = x.shape
    d_ff = hist_f.shape[-1] // 2
    wf = hist_f.shape[1] + 1
    nj = d_ff // tj
    nt = seq // lt
    m = sb * lt
    pad = _round_up(wf - 1, SUBLANES)
    assert lt >= pad
    row3 = lambda s, t, j: (s, t, 0)
    fixed2 = lambda s, t, j: (0, 0)
    gate2 = lambda s, t, j: (0, j)
    val2 = lambda s, t, j: (0, nj + j)
    gate3 = lambda s, t, j: (s, 0, j)
    val3 = lambda s, t, j: (s, 0, nj + j)
    new3 = lambda s, t, j: (s, 0, jnp.where(t == nt - 1, j, 0))
    kern = functools.partial(_ffn_kernel, sb=sb, lt=lt, nj=nj, wf=wf)
    y, new_g, new_v = pl.pallas_call(
        kern,
        grid=(ns // sb, nt, nj),
        in_specs=[
            pl.BlockSpec((sb, lt, d_model), row3),
            pl.BlockSpec((sb, wf - 1, tj), gate3),
            pl.BlockSpec((sb, wf - 1, tj), val3),
            pl.BlockSpec((1, d_model), fixed2),
            pl.BlockSpec((d_model, tj), gate2),
            pl.BlockSpec((d_model, tj), val2),
            pl.BlockSpec((wf, tj), gate2),
            pl.BlockSpec((wf, tj), val2),
            pl.BlockSpec((tj, d_model), lambda s, t, j: (j, 0)),
            pl.BlockSpec((1, d_model), fixed2),
        ],
        out_specs=[
            pl.BlockSpec((sb, lt, d_model), row3),
            pl.BlockSpec((sb, wf - 1, tj), new3),
            pl.BlockSpec((sb, wf - 1, tj), new3),
        ],
        out_shape=[
            jax.ShapeDtypeStruct(x.shape, F32),
            jax.ShapeDtypeStruct((ns, wf - 1, d_ff), F32),
            jax.ShapeDtypeStruct((ns, wf - 1, d_ff), F32),
        ],
        scratch_shapes=[
            pltpu.VMEM((m, d_model), BF16),
            pltpu.VMEM((m, d_model), F32),
            pltpu.VMEM((2 * nj, sb, pad, tj), F32),
            pltpu.VMEM((sb, pad + lt, tj), F32),
            pltpu.VMEM((sb, pad + lt, tj), F32),
        ],
        compiler_params=pltpu.CompilerParams(
            dimension_semantics=("arbitrary", "arbitrary", "arbitrary"),
            vmem_limit_bytes=VMEM_LIMIT_BYTES),
        name="ffn",
    )(x, hist_f, hist_f, w["g_pre_ffn"], w["w_up"], w["w_up"], w["ffn_conv_w"], w["ffn_conv_w"],
      w["w_down"], w["g_post_ffn"])
    return y, jnp.concatenate([new_g, new_v], axis=-1)


def _layer(x, hist_a, hist_b, hist_f, w, *, sb, lt):
    act_a, act_b, new_a, new_b = _mix_in(x, hist_a, hist_b, w, sb=sb, lt=lt, tc=256)
    x1 = _mix_out(x, act_a, act_b, w, sb=sb, lt=lt, tn=256)
    y, new_f = _ffn(x1, hist_f, w, sb=sb, lt=lt, tj=512)
    return y, new_a, new_b, new_f


def kernel(x_prompt, x_sample, state_conf_conv, state_sconv, state_ffn_conv, meta_tokens, g_pre_mix, w_in, conf_conv_w, conf_conv_b, conf_ln_g, conf_ln_b, w_conf_out, sconv_w, w_sconv_out, w_o, g_post_mix, g_pre_ffn, w_up, ffn_conv_w, w_down, g_post_ffn):
    depth = w_in.shape[0]
    batch, seq, _ = x_prompt.shape
    dec_batch, dec_seq, _ = x_sample.shape
    n_meta = meta_tokens.shape[0]
    prompt_tile = 512
    assert seq % prompt_tile == 0

    xm = meta_tokens.astype(x_prompt.dtype)[None]
    xp = x_prompt
    xs = x_sample
    outs = [[] for _ in range(6)]
    for l in range(depth):
        w = {
            "g_pre_mix": g_pre_mix[l][None], "w_in": w_in[l].astype(BF16),
            "conf_conv_w": conf_conv_w[l], "conf_conv_b": conf_conv_b[l][None],
            "conf_ln_g": conf_ln_g[l][None], "conf_ln_b": conf_ln_b[l][None],
            "w_conf_out": w_conf_out[l].astype(BF16), "sconv_w": sconv_w[l],
            "w_sconv_out": w_sconv_out[l].astype(BF16), "w_o": w_o[l].astype(BF16),
            "g_post_mix": g_post_mix[l][None], "g_pre_ffn": g_pre_ffn[l][None],
            "w_up": w_up[l].astype(BF16), "ffn_conv_w": ffn_conv_w[l],
            "w_down": w_down[l].astype(BF16), "g_post_ffn": g_post_ffn[l][None],
        }
        zeros = lambda ref: jnp.zeros((1,) + ref.shape[2:], x_prompt.dtype)
        xm, ma, mb, mf = _layer(xm, zeros(state_conf_conv), zeros(state_sconv), zeros(state_ffn_conv),
                                w, sb=1, lt=n_meta)
        bcast = lambda hist: jnp.broadcast_to(hist, (batch,) + hist.shape[1:])
        xp, pa, pb, pf = _layer(xp, bcast(ma), bcast(mb), bcast(mf), w, sb=1, lt=prompt_tile)
        xs, sa, sb_, sf = _layer(xs, state_conf_conv[l], state_sconv[l], state_ffn_conv[l], w,
                                 sb=dec_batch, lt=dec_seq)
        for acc, val in zip(outs, (pa, pb, pf, sa, sb_, sf)):
            acc.append(val)
    return (xp, xs) + tuple(jnp.stack(o) for o in outs)
```

```python
import functools

import jax
import jax.numpy as jnp
from jax import lax
from jax.experimental import pallas as pl
from jax.experimental.pallas import tpu as pltpu

EPS = 1e-6
N_SPLIT_IN = 9
SUBLANES = 8
CONV_ROW_CHUNK = 32
MIX_IN_CHUNK_ROWS = 128
NORM_CHUNK_ROWS = 16
V7X_VMEM_BYTES = 64 * 1024 * 1024
VMEM_LIMIT_BYTES = V7X_VMEM_BYTES - 8 * 1024 * 1024

F32 = jnp.float32
BF16 = jnp.bfloat16


def _round_up(n, m):
    return (n + m - 1) // m * m


def _rmsnorm_rows(xf, g):
    ms = jnp.mean(xf * xf, axis=-1, keepdims=True)
    return xf * lax.rsqrt(ms + EPS) * g


def _prenorm_to_scratch(h_scr, x_ref, g_ref, sb, lt):
    for s0, s1, r0, r1 in _row_chunks(sb, lt, NORM_CHUNK_ROWS):
        rows = (s1 - s0) * (r1 - r0)
        xf = x_ref[s0:s1, r0:r1, :].reshape(rows, x_ref.shape[-1])
        row0 = s0 * lt + r0
        h_scr[row0:row0 + rows, :] = _rmsnorm_rows(xf, g_ref[...]).astype(BF16)


def _postnorm_residual(out_ref, x_ref, acc_scr, g_ref, sb, lt):
    for s0, s1, r0, r1 in _row_chunks(sb, lt, NORM_CHUNK_ROWS):
        rows = (s1 - s0) * (r1 - r0)
        row0 = s0 * lt + r0
        y = _rmsnorm_rows(acc_scr[row0:row0 + rows, :], g_ref[...])
        out_ref[s0:s1, r0:r1, :] = x_ref[s0:s1, r0:r1, :] + y.reshape(s1 - s0, r1 - r0, y.shape[-1])


def _dot(a, b):
    return jnp.dot(a, b, preferred_element_type=F32)


def _row_chunks(sb, lt, rows):
    if lt >= rows:
        assert lt % rows == 0
        return [(s, s + 1, r0, r0 + rows) for s in range(sb) for r0 in range(0, lt, rows)]
    per = max(1, rows // lt)
    return [(s0, min(s0 + per, sb), 0, lt) for s0 in range(0, sb, per)]


def _mix_in_kernel(x_ref, hista_ref, histb_ref, gpre_ref, wv_ref, wg_ref, wb_ref, wc_ref, wx_ref,
                   cw_ref, cb_ref, lng_ref, lnb_ref, sw_ref,
                   acta_ref, actb_ref, newa_ref, newb_ref,
                   h_scr, xpa_scr, xpu_scr, aconv_scr, ph_scr, *, sb, lt, nc, tc, wa, wu):
    t = pl.program_id(1)
    c = pl.program_id(2)
    m = sb * lt
    d_model = x_ref.shape[-1]
    pad_a = xpa_scr.shape[2] - lt
    pad_u = xpu_scr.shape[2] - lt

    @pl.when(c == 0)
    def _():
        _prenorm_to_scratch(h_scr, x_ref, gpre_ref, sb, lt)

    @pl.when((t == 0) & (c == 0))
    def _():
        for cc in range(nc):
            xpa_scr[cc, :, pad_a - (wa - 1):pad_a, :] = hista_ref[:, :, cc * tc:(cc + 1) * tc]
            xpu_scr[cc, :, pad_u - (wu - 1):pad_u, :] = histb_ref[:, :, cc * tc:(cc + 1) * tc]

    rc = min(lt, CONV_ROW_CHUNK)
    for s0, s1, r0, r1 in _row_chunks(sb, lt, MIX_IN_CHUNK_ROWS):
        nr = r1 - r0
        row0 = s0 * lt + r0
        hc = h_scr[row0:row0 + (s1 - s0) * nr, :]
        shape3 = (s1 - s0, nr, tc)
        a = _dot(hc, wv_ref[...]) * jax.nn.sigmoid(_dot(hc, wg_ref[...]))
        xpa_scr[c, s0:s1, pad_a + r0:pad_a + r1, :] = a.reshape(shape3)
        lo = 0 if r0 == 0 else r0 + pad_a - SUBLANES
        hi = r1 + pad_a - SUBLANES
        for p in range(1, SUBLANES):
            ph_scr[p - 1, s0:s1, lo:hi, :] = xpa_scr[c, s0:s1, lo + p:hi + p, :]
        for s in range(s0, s1):
            for q0 in range(r0, r1, rc):
                acc = None
                for k in range(wa):
                    q, p = divmod(pad_a - (wa - 1) + k, SUBLANES)
                    at = q * SUBLANES + q0
                    if p == 0:
                        src = xpa_scr[c, s, at:at + rc, :]
                    else:
                        src = ph_scr[p - 1, s, at:at + rc, :]
                    term = cw_ref[k:k + 1, :] * src
                    acc = term if acc is None else acc + term
                aconv_scr[c, s, q0:q0 + rc, :] = acc + cb_ref[...]
        u = _dot(hc, wc_ref[...]) * _dot(hc, wx_ref[...])
        xpu_scr[c, s0:s1, pad_u + r0:pad_u + r1, :] = u.reshape(shape3)
        uconv = None
        for k in range(wu):
            at = pad_u - (wu - 1) + k + r0
            term = sw_ref[k:k + 1, :] * xpu_scr[c, s0:s1, at:at + nr, :]
            uconv = term if uconv is None else uconv + term
        s_b = _dot(hc, wb_ref[...]).reshape(shape3)
        actb_ref[s0:s1, r0:r1, :] = (s_b * uconv).astype(BF16)
    xpa_scr[c, :, pad_a - (wa - 1):pad_a, :] = xpa_scr[c, :, pad_a + lt - (wa - 1):pad_a + lt, :]
    xpu_scr[c, :, pad_u - (wu - 1):pad_u, :] = xpu_scr[c, :, pad_u + lt - (wu - 1):pad_u + lt, :]

    @pl.when(c == nc - 1)
    def _():
        d_conf = nc * tc
        tot = None
        for cc in range(nc):
            part = jnp.sum(aconv_scr[cc], axis=-1, keepdims=True)
            tot = part if tot is None else tot + part
        mu = tot / d_conf
        sq = None
        for cc in range(nc):
            xc = aconv_scr[cc] - mu
            part = jnp.sum(xc * xc, axis=-1, keepdims=True)
            sq = part if sq is None else sq + part
        rstd = lax.rsqrt(sq / d_conf + EPS)
        for cc in range(nc):
            sl = slice(cc * tc, (cc + 1) * tc)
            y = (aconv_scr[cc] - mu) * rstd * lng_ref[:, sl] + lnb_ref[:, sl]
            acta_ref[:, :, sl] = (y * jax.nn.sigmoid(y)).astype(BF16)
            newa_ref[:, :, sl] = xpa_scr[cc, :, pad_a - (wa - 1):pad_a, :]
            newb_ref[:, :, sl] = xpu_scr[cc, :, pad_u - (wu - 1):pad_u, :]


def _mix_in(x, hist_a, hist_b, w, *, sb, lt, tc):
    ns, seq, d_model = x.shape
    d_conf = hist_a.shape[-1]
    wa = hist_a.shape[1] + 1
    wu = hist_b.shape[1] + 1
    nc = d_conf // tc
    nt = seq // lt
    m = sb * lt
    pad_a = _round_up(wa - 1, SUBLANES)
    pad_u = _round_up(wu - 1, SUBLANES)
    grid = (ns // sb, nt, nc)

    def w_in_spec(group):
        return pl.BlockSpec((d_model, tc), lambda s, t, c, g=group: (0, g * nc + c))

    row3 = lambda s, t, c: (s, t, 0)
    stream3 = lambda s, t, c: (s, 0, 0)
    chan2 = lambda s, t, c: (0, c)
    fixed2 = lambda s, t, c: (0, 0)
    kern = functools.partial(_mix_in_kernel, sb=sb, lt=lt, nc=nc, tc=tc, wa=wa, wu=wu)
    return pl.pallas_call(
        kern,
        grid=grid,
        in_specs=[
            pl.BlockSpec((sb, lt, d_model), row3),
            pl.BlockSpec((sb, wa - 1, d_conf), stream3),
            pl.BlockSpec((sb, wu - 1, d_conf), stream3),
            pl.BlockSpec((1, d_model), fixed2),
            w_in_spec(0), w_in_spec(1), w_in_spec(2), w_in_spec(3), w_in_spec(4),
            pl.BlockSpec((wa, tc), chan2),
            pl.BlockSpec((1, tc), chan2),
            pl.BlockSpec((1, d_conf), fixed2),
            pl.BlockSpec((1, d_conf), fixed2),
            pl.BlockSpec((wu, tc), chan2),
        ],
        out_specs=[
            pl.BlockSpec((sb, lt, d_conf), row3),
            pl.BlockSpec((sb, lt, tc), lambda s, t, c: (s, t, c)),
            pl.BlockSpec((sb, wa - 1, d_conf), stream3),
            pl.BlockSpec((sb, wu - 1, d_conf), stream3),
        ],
        out_shape=[
            jax.ShapeDtypeStruct((ns, seq, d_conf), BF16),
            jax.ShapeDtypeStruct((ns, seq, d_conf), BF16),
            jax.ShapeDtypeStruct(hist_a.shape, F32),
            jax.ShapeDtypeStruct(hist_b.shape, F32),
        ],
        scratch_shapes=[
            pltpu.VMEM((m, d_model), BF16),
            pltpu.VMEM((nc, sb, pad_a + lt, tc), F32),
            pltpu.VMEM((nc, sb, pad_u + lt, tc), F32),
            pltpu.VMEM((nc, sb, lt, tc), F32),
            pltpu.VMEM((SUBLANES - 1, sb, lt + pad_a - SUBLANES, tc), F32),
        ],
        compiler_params=pltpu.CompilerParams(
            dimension_semantics=("arbitrary", "arbitrary", "arbitrary"),
            vmem_limit_bytes=VMEM_LIMIT_BYTES),
        name="mix_in",
    )(x, hist_a, hist_b, w["g_pre_mix"], w["w_in"], w["w_in"], w["w_in"], w["w_in"], w["w_in"],
      w["conf_conv_w"], w["conf_conv_b"], w["conf_ln_g"], w["conf_ln_b"], w["sconv_w"])


def _mix_out_kernel(x_ref, acta_ref, actb_ref, gpre_ref, wco_ref, wso_ref, wga_ref, wgb_ref,
                    wo_ref, gpost_ref, out_ref, h_scr, acc_scr, *, sb, lt, nn):
    n = pl.program_id(2)
    m = sb * lt

    @pl.when(n == 0)
    def _():
        _prenorm_to_scratch(h_scr, x_ref, gpre_ref, sb, lt)
        acc_scr[...] = jnp.zeros_like(acc_scr)

    h = h_scr[...]
    a_out = _dot(acta_ref[...].reshape(m, -1), wco_ref[...])
    b_out = _dot(actb_ref[...].reshape(m, -1), wso_ref[...])
    gate_a = jax.nn.sigmoid(_dot(h, wga_ref[...]))
    gate_b = jax.nn.sigmoid(_dot(h, wgb_ref[...]))
    merged = (gate_a * a_out + gate_b * b_out).astype(BF16)
    acc_scr[...] += _dot(merged, wo_ref[...])

    @pl.when(n == nn - 1)
    def _():
        _postnorm_residual(out_ref, x_ref, acc_scr, gpost_ref, sb, lt)


def _mix_out(x, act_a, act_b, w, *, sb, lt, tn):
    ns, seq, d_model = x.shape
    d_conf = act_a.shape[-1]
    nn = d_model // tn
    nt = seq // lt
    m = sb * lt
    gate_a_blk = 5 * d_conf // tn
    gate_b_blk = gate_a_blk + nn
    row3 = lambda s, t, n: (s, t, 0)
    fixed2 = lambda s, t, n: (0, 0)
    col2 = lambda s, t, n: (0, n)
    kern = functools.partial(_mix_out_kernel, sb=sb, lt=lt, nn=nn)
    return pl.pallas_call(
        kern,
        grid=(ns // sb, nt, nn),
        in_specs=[
            pl.BlockSpec((sb, lt, d_model), row3),
            pl.BlockSpec((sb, lt, d_conf), row3),
            pl.BlockSpec((sb, lt, d_conf), row3),
            pl.BlockSpec((1, d_model), fixed2),
            pl.BlockSpec((d_conf, tn), col2),
            pl.BlockSpec((d_conf, tn), col2),
            pl.BlockSpec((d_model, tn), lambda s, t, n: (0, gate_a_blk + n)),
            pl.BlockSpec((d_model, tn), lambda s, t, n: (0, gate_b_blk + n)),
            pl.BlockSpec((tn, d_model), lambda s, t, n: (n, 0)),
            pl.BlockSpec((1, d_model), fixed2),
        ],
        out_specs=pl.BlockSpec((sb, lt, d_model), row3),
        out_shape=jax.ShapeDtypeStruct(x.shape, F32),
        scratch_shapes=[
            pltpu.VMEM((m, d_model), BF16),
            pltpu.VMEM((m, d_model), F32),
        ],
        compiler_params=pltpu.CompilerParams(
            dimension_semantics=("arbitrary", "arbitrary", "arbitrary"),
            vmem_limit_bytes=VMEM_LIMIT_BYTES),
        name="mix_out",
    )(x, act_a, act_b, w["g_pre_mix"], w["w_conf_out"], w["w_sconv_out"], w["w_in"], w["w_in"],
      w["w_o"], w["g_post_mix"])


def _ffn_kernel(x_ref, histg_ref, histv_ref, gpre_ref, wug_ref, wuv_ref, cwg_ref, cwv_ref,
                wd_ref, gpost_ref, out_ref, newg_ref, newv_ref,
                h_scr, acc_scr, carry_scr, xpg_scr, xpv_scr, *, sb, lt, nj, wf):
    t = pl.program_id(1)
    j = pl.program_id(2)
    m = sb * lt
    d_model = x_ref.shape[-1]
    tj = wug_ref.shape[-1]
    pad = xpg_scr.shape[1] - lt

    @pl.when(j == 0)
    def _():
        _prenorm_to_scratch(h_scr, x_ref, gpre_ref, sb, lt)
        acc_scr[...] = jnp.zeros_like(acc_scr)

    @pl.when(t == 0)
    def _():
        for slot, hist_ref in ((j, histg_ref), (nj + j, histv_ref)):
            carry_scr[slot] = jnp.zeros(carry_scr.shape[1:], F32)
            carry_scr[slot, :, pad - (wf - 1):pad, :] = hist_ref[...]

    h = h_scr[...]

    def conv(up, cw_ref, new_ref, slot, xp_scr):
        xp_scr[:, 0:pad, :] = carry_scr[slot]
        xp_scr[:, pad:pad + lt, :] = up.reshape(sb, lt, tj)
        y = None
        for k in range(wf):
            lo = pad - (wf - 1) + k
            term = cw_ref[k:k + 1, :] * xp_scr[:, lo:lo + lt, :]
            y = term if y is None else y + term
        carry_scr[slot] = xp_scr[:, lt:lt + pad, :]
        new_ref[...] = xp_scr[:, pad + lt - (wf - 1):pad + lt, :]
        return y.reshape(m, tj)

    f_gate = conv(_dot(h, wug_ref[...]), cwg_ref, newg_ref, j, xpg_scr)
    f_val = conv(_dot(h, wuv_ref[...]), cwv_ref, newv_ref, nj + j, xpv_scr)
    act = (f_gate * jax.nn.sigmoid(f_gate) * f_val).astype(BF16)
    acc_scr[...] += _dot(act, wd_ref[...])

    @pl.when(j == nj - 1)
    def _():
        _postnorm_residual(out_ref, x_ref, acc_scr, gpost_ref, sb, lt)


def _ffn(x, hist_f, w, *, sb, lt, tj):
    ns, seq, d_model = x.shape
    d_ff = hist_f.shape[-1] // 2
    wf = hist_f.shape[1] + 1
    nj = d_ff // tj
    nt = seq // lt
    m = sb * lt
    pad = _round_up(wf - 1, SUBLANES)
    assert lt >= pad
    row3 = lambda s, t, j: (s, t, 0)
    fixed2 = lambda s, t, j: (0, 0)
    gate2 = lambda s, t, j: (0, j)
    val2 = lambda s, t, j: (0, nj + j)
    gate3 = lambda s, t, j: (s, 0, j)
    val3 = lambda s, t, j: (s, 0, nj + j)
    new3 = lambda s, t, j: (s, 0, jnp.where(t == nt - 1, j, 0))
    kern = functools.partial(_ffn_kernel, sb=sb, lt=lt, nj=nj, wf=wf)
    y, new_g, new_v = pl.pallas_call(
        kern,
        grid=(ns // sb, nt, nj),
        in_specs=[
            pl.BlockSpec((sb, lt, d_model), row3),
            pl.BlockSpec((sb, wf - 1, tj), gate3),
            pl.BlockSpec((sb, wf - 1, tj), val3),
            pl.BlockSpec((1, d_model), fixed2),
            pl.BlockSpec((d_model, tj), gate2),
            pl.BlockSpec((d_model, tj), val2),
            pl.BlockSpec((wf, tj), gate2),
            pl.BlockSpec((wf, tj), val2),
            pl.BlockSpec((tj, d_model), lambda s, t, j: (j, 0)),
            pl.BlockSpec((1, d_model), fixed2),
        ],
        out_specs=[
            pl.BlockSpec((sb, lt, d_model), row3),
            pl.BlockSpec((sb, wf - 1, tj), new3),
            pl.BlockSpec((sb, wf - 1, tj), new3),
        ],
        out_shape=[
            jax.ShapeDtypeStruct(x.shape, F32),
            jax.ShapeDtypeStruct((ns, wf - 1, d_ff), F32),
            jax.ShapeDtypeStruct((ns, wf - 1, d_ff), F32),
        ],
        scratch_shapes=[
            pltpu.VMEM((m, d_model), BF16),
            pltpu.VMEM((m, d_model), F32),
            pltpu.VMEM((2 * nj, sb, pad, tj), F32),
            pltpu.VMEM((sb, pad + lt, tj), F32),
            pltpu.VMEM((sb, pad + lt, tj), F32),
        ],
        compiler_params=pltpu.CompilerParams(
            dimension_semantics=("arbitrary", "arbitrary", "arbitrary"),
            vmem_limit_bytes=VMEM_LIMIT_BYTES),
        name="ffn",
    )(x, hist_f, hist_f, w["g_pre_ffn"], w["w_up"], w["w_up"], w["ffn_conv_w"], w["ffn_conv_w"],
      w["w_down"], w["g_post_ffn"])
    return y, jnp.concatenate([new_g, new_v], axis=-1)


def _layer(x, hist_a, hist_b, hist_f, w, *, sb, lt, lead_pad=0):
    def zero_pad(v):
        return v.at[-1, :lead_pad].set(0.0) if lead_pad else v

    act_a, act_b, new_a, new_b = _mix_in(x, hist_a, hist_b, w, sb=sb, lt=lt, tc=256)
    x1 = zero_pad(_mix_out(x, act_a, act_b, w, sb=sb, lt=lt, tn=512))
    y, new_f = _ffn(x1, hist_f, w, sb=sb, lt=lt, tj=512)
    return zero_pad(y), new_a, new_b, new_f


def kernel(x_prompt, x_sample, state_conf_conv, state_sconv, state_ffn_conv, meta_tokens, g_pre_mix, w_in, conf_conv_w, conf_conv_b, conf_ln_g, conf_ln_b, w_conf_out, sconv_w, w_sconv_out, w_o, g_post_mix, g_pre_ffn, w_up, ffn_conv_w, w_down, g_post_ffn):
    depth = w_in.shape[0]
    batch, seq, _ = x_prompt.shape
    dec_batch, dec_seq, _ = x_sample.shape
    n_meta = meta_tokens.shape[0]
    prompt_tile = 512
    assert seq % prompt_tile == 0

    assert n_meta <= dec_seq
    lead_pad = dec_seq - n_meta
    xm = jnp.pad(meta_tokens.astype(x_prompt.dtype), ((lead_pad, 0), (0, 0)))[None]
    xp = x_prompt
    xs = jnp.concatenate([x_sample, xm], axis=0)
    outs = [[] for _ in range(6)]
    for l in range(depth):
        w = {
            "g_pre_mix": g_pre_mix[l][None], "w_in": w_in[l].astype(BF16),
            "conf_conv_w": conf_conv_w[l], "conf_conv_b": conf_conv_b[l][None],
            "conf_ln_g": conf_ln_g[l][None], "conf_ln_b": conf_ln_b[l][None],
            "w_conf_out": w_conf_out[l].astype(BF16), "sconv_w": sconv_w[l],
            "w_sconv_out": w_sconv_out[l].astype(BF16), "w_o": w_o[l].astype(BF16),
            "g_post_mix": g_post_mix[l][None], "g_pre_ffn": g_pre_ffn[l][None],
            "w_up": w_up[l].astype(BF16), "ffn_conv_w": ffn_conv_w[l],
            "w_down": w_down[l].astype(BF16), "g_post_ffn": g_post_ffn[l][None],
        }
        with_meta = lambda st: jnp.pad(st[l], ((0, 1), (0, 0), (0, 0)))
        xs, sa, sb_, sf = _layer(xs, with_meta(state_conf_conv), with_meta(state_sconv),
                                 with_meta(state_ffn_conv), w, sb=dec_batch + 1, lt=dec_seq,
                                 lead_pad=lead_pad)
        bcast = lambda hist: jnp.broadcast_to(hist[dec_batch:], (batch,) + hist.shape[1:])
        xp, pa, pb, pf = _layer(xp, bcast(sa), bcast(sb_), bcast(sf), w, sb=1, lt=prompt_tile)
        for acc, val in zip(outs, (pa, pb, pf, sa[:dec_batch], sb_[:dec_batch], sf[:dec_batch])):
            acc.append(val)
    return (xp, xs[:dec_batch]) + tuple(jnp.stack(o) for o in outs)
```

```python
import functools

import jax
import jax.numpy as jnp
from jax import lax
from jax.experimental import pallas as pl
from jax.experimental.pallas import tpu as pltpu

EPS = 1e-6
SUBLANES = 8
CONV_ROW_CHUNK = 32
MIX_IN_CHUNK_ROWS = 256
NORM_CHUNK_ROWS = 16
V7X_VMEM_BYTES = 64 * 1024 * 1024
VMEM_LIMIT_BYTES = V7X_VMEM_BYTES - 8 * 1024 * 1024

F32 = jnp.float32
BF16 = jnp.bfloat16


def _round_up(n, m):
    return (n + m - 1) // m * m


def _rmsnorm_rows(xf, g):
    ms = jnp.mean(xf * xf, axis=-1, keepdims=True)
    return xf * lax.rsqrt(ms + EPS) * g


def _prenorm_to_scratch(h_scr, x_ref, g_ref, sb, lt):
    for s0, s1, r0, r1 in _row_chunks(sb, lt, NORM_CHUNK_ROWS):
        rows = (s1 - s0) * (r1 - r0)
        xf = x_ref[s0:s1, r0:r1, :].reshape(rows, x_ref.shape[-1])
        row0 = s0 * lt + r0
        h_scr[row0:row0 + rows, :] = _rmsnorm_rows(xf, g_ref[...]).astype(BF16)


def _postnorm_residual(out_ref, x_ref, acc_scr, g_ref, sb, lt):
    for s0, s1, r0, r1 in _row_chunks(sb, lt, NORM_CHUNK_ROWS):
        rows = (s1 - s0) * (r1 - r0)
        row0 = s0 * lt + r0
        y = _rmsnorm_rows(acc_scr[row0:row0 + rows, :], g_ref[...])
        out_ref[s0:s1, r0:r1, :] = x_ref[s0:s1, r0:r1, :] + y.reshape(s1 - s0, r1 - r0, y.shape[-1])


def _dot(a, b):
    return jnp.dot(a, b, preferred_element_type=F32)


def _bf16_weight_tiles(w_refs, wout_refs):
    for src, dst in zip(w_refs, wout_refs):
        dst[...] = src[...].astype(BF16)
    return wout_refs


def _col_tiled(w, name, rows, tile):
    arr, off = w[name]
    assert off % tile == 0
    blk = off // tile
    return arr, pl.BlockSpec((rows, tile), lambda s, t, i, b=blk: (0, b + i))


def _row_chunks(sb, lt, rows):
    if lt >= rows:
        assert lt % rows == 0
        return [(s, s + 1, r0, r0 + rows) for s in range(sb) for r0 in range(0, lt, rows)]
    per = max(1, rows // lt)
    return [(s0, min(s0 + per, sb), 0, lt) for s0 in range(0, sb, per)]


def _mix_in_kernel(x_ref, hista_ref, histb_ref, gpre_ref, wv_ref, wg_ref, wb_ref, wc_ref, wx_ref,
                   cw_ref, cb_ref, lng_ref, lnb_ref, sw_ref,
                   acta_ref, actb_ref, newa_ref, newb_ref, *rest, sb, lt, nc, tc, wa, wu, emit):
    if emit:
        wv_ref, wg_ref, wb_ref, wc_ref, wx_ref = _bf16_weight_tiles(
            (wv_ref, wg_ref, wb_ref, wc_ref, wx_ref), rest[:5])
        rest = rest[5:]
    h_scr, xpa_scr, xpu_scr, aconv_scr, ph_scr = rest
    t = pl.program_id(1)
    c = pl.program_id(2)
    pad_a = xpa_scr.shape[2] - lt
    pad_u = xpu_scr.shape[2] - lt

    @pl.when(c == 0)
    def _():
        _prenorm_to_scratch(h_scr, x_ref, gpre_ref, sb, lt)

    @pl.when((t == 0) & (c == 0))
    def _():
        for cc in range(nc):
            xpa_scr[cc, :, pad_a - (wa - 1):pad_a, :] = hista_ref[:, :, cc * tc:(cc + 1) * tc]
            xpu_scr[cc, :, pad_u - (wu - 1):pad_u, :] = histb_ref[:, :, cc * tc:(cc + 1) * tc]

    rc = min(lt, CONV_ROW_CHUNK)
    for s0, s1, r0, r1 in _row_chunks(sb, lt, MIX_IN_CHUNK_ROWS):
        nr = r1 - r0
        row0 = s0 * lt + r0
        hc = h_scr[row0:row0 + (s1 - s0) * nr, :]
        shape3 = (s1 - s0, nr, tc)
        a = _dot(hc, wv_ref[...]) * jax.nn.sigmoid(_dot(hc, wg_ref[...]))
        xpa_scr[c, s0:s1, pad_a + r0:pad_a + r1, :] = a.reshape(shape3)
        lo = 0 if r0 == 0 else r0 + pad_a - SUBLANES
        hi = r1 + pad_a - SUBLANES
        for p in range(1, SUBLANES):
            ph_scr[p - 1, s0:s1, lo:hi, :] = xpa_scr[c, s0:s1, lo + p:hi + p, :]
        for s in range(s0, s1):
            for q0 in range(r0, r1, rc):
                acc = None
                for k in range(wa):
                    q, p = divmod(pad_a - (wa - 1) + k, SUBLANES)
                    at = q * SUBLANES + q0
                    if p == 0:
                        src = xpa_scr[c, s, at:at + rc, :]
                    else:
                        src = ph_scr[p - 1, s, at:at + rc, :]
                    term = cw_ref[k][None] * src.reshape(rc // SUBLANES, SUBLANES, tc)
                    acc = term if acc is None else acc + term
                aconv_scr[c, s, q0:q0 + rc, :] = acc.reshape(rc, tc) + cb_ref[...]
        u = _dot(hc, wc_ref[...]) * _dot(hc, wx_ref[...])
        xpu_scr[c, s0:s1, pad_u + r0:pad_u + r1, :] = u.reshape(shape3)
        uconv = None
        for k in range(wu):
            at = pad_u - (wu - 1) + k + r0
            term = sw_ref[k:k + 1, :] * xpu_scr[c, s0:s1, at:at + nr, :]
            uconv = term if uconv is None else uconv + term
        s_b = _dot(hc, wb_ref[...]).reshape(shape3)
        actb_ref[s0:s1, r0:r1, :] = (s_b * uconv).astype(BF16)
    xpa_scr[c, :, pad_a - (wa - 1):pad_a, :] = xpa_scr[c, :, pad_a + lt - (wa - 1):pad_a + lt, :]
    xpu_scr[c, :, pad_u - (wu - 1):pad_u, :] = xpu_scr[c, :, pad_u + lt - (wu - 1):pad_u + lt, :]

    @pl.when(c == nc - 1)
    def _():
        d_conf = nc * tc
        tot = None
        for cc in range(nc):
            part = jnp.sum(aconv_scr[cc], axis=-1, keepdims=True)
            tot = part if tot is None else tot + part
        mu = tot / d_conf
        sq = None
        for cc in range(nc):
            xc = aconv_scr[cc] - mu
            part = jnp.sum(xc * xc, axis=-1, keepdims=True)
            sq = part if sq is None else sq + part
        rstd = lax.rsqrt(sq / d_conf + EPS)
        for cc in range(nc):
            sl = slice(cc * tc, (cc + 1) * tc)
            y = (aconv_scr[cc] - mu) * rstd * lng_ref[:, sl] + lnb_ref[:, sl]
            acta_ref[:, :, sl] = (y * jax.nn.sigmoid(y)).astype(BF16)
            newa_ref[:, :, sl] = xpa_scr[cc, :, pad_a - (wa - 1):pad_a, :]
            newb_ref[:, :, sl] = xpu_scr[cc, :, pad_u - (wu - 1):pad_u, :]


def _mix_in(x, hist_a, hist_b, w, *, sb, lt, tc, emit):
    ns, seq, d_model = x.shape
    d_conf = hist_a.shape[-1]
    wa = hist_a.shape[1] + 1
    wu = hist_b.shape[1] + 1
    nc = d_conf // tc
    nt = seq // lt
    m = sb * lt
    pad_a = _round_up(wa - 1, SUBLANES)
    pad_u = _round_up(wu - 1, SUBLANES)
    grid = (ns // sb, nt, nc)
    assert not emit or grid[:2] == (1, 1)
    w_names = ("w_val", "w_gate", "w_b", "w_c", "w_x")
    w_arrs, w_specs = zip(*[_col_tiled(w, name, d_model, tc) for name in w_names])
    row3 = lambda s, t, c: (s, t, 0)
    stream3 = lambda s, t, c: (s, 0, 0)
    chan2 = lambda s, t, c: (0, c)
    fixed2 = lambda s, t, c: (0, 0)
    kern = functools.partial(_mix_in_kernel, sb=sb, lt=lt, nc=nc, tc=tc, wa=wa, wu=wu, emit=emit)
    n_emit = len(w_names) if emit else 0
    res = pl.pallas_call(
        kern,
        grid=grid,
        in_specs=[
            pl.BlockSpec((sb, lt, d_model), row3),
            pl.BlockSpec((sb, wa - 1, d_conf), stream3),
            pl.BlockSpec((sb, wu - 1, d_conf), stream3),
            pl.BlockSpec((1, d_model), fixed2),
            *w_specs,
            pl.BlockSpec((wa, SUBLANES, tc), lambda s, t, c: (0, 0, c)),
            pl.BlockSpec((1, tc), chan2),
            pl.BlockSpec((1, d_conf), fixed2),
            pl.BlockSpec((1, d_conf), fixed2),
            pl.BlockSpec((wu, tc), chan2),
        ],
        out_specs=[
            pl.BlockSpec((sb, lt, d_conf), row3),
            pl.BlockSpec((sb, lt, tc), lambda s, t, c: (s, t, c)),
            pl.BlockSpec((sb, wa - 1, d_conf), stream3),
            pl.BlockSpec((sb, wu - 1, d_conf), stream3),
        ] + [pl.BlockSpec((d_model, tc), chan2)] * n_emit,
        out_shape=[
            jax.ShapeDtypeStruct((ns, seq, d_conf), BF16),
            jax.ShapeDtypeStruct((ns, seq, d_conf), BF16),
            jax.ShapeDtypeStruct(hist_a.shape, F32),
            jax.ShapeDtypeStruct(hist_b.shape, F32),
        ] + [jax.ShapeDtypeStruct((d_model, d_conf), BF16)] * n_emit,
        scratch_shapes=[
            pltpu.VMEM((m, d_model), BF16),
            pltpu.VMEM((nc, sb, pad_a + lt, tc), F32),
            pltpu.VMEM((nc, sb, pad_u + lt, tc), F32),
            pltpu.VMEM((nc, sb, lt, tc), F32),
            pltpu.VMEM((SUBLANES - 1, sb, lt + pad_a - SUBLANES, tc), F32),
        ],
        compiler_params=pltpu.CompilerParams(
            dimension_semantics=("arbitrary", "arbitrary", "arbitrary"),
            vmem_limit_bytes=VMEM_LIMIT_BYTES),
        name="mix_in",
    )(x, hist_a, hist_b, w["g_pre_mix"], *w_arrs,
      w["conf_conv_w"], w["conf_conv_b"], w["conf_ln_g"], w["conf_ln_b"], w["sconv_w"])
    return res[:4], {name: (arr, 0) for name, arr in zip(w_names, res[4:])}


def _mix_out_kernel(x_ref, acta_ref, actb_ref, gpre_ref, wco_ref, wso_ref, wga_ref, wgb_ref,
                    wo_ref, gpost_ref, out_ref, *rest, sb, lt, nn, emit):
    if emit:
        wco_ref, wso_ref, wga_ref, wgb_ref, wo_ref = _bf16_weight_tiles(
            (wco_ref, wso_ref, wga_ref, wgb_ref, wo_ref), rest[:5])
        rest = rest[5:]
    h_scr, acc_scr = rest
    n = pl.program_id(2)
    m = sb * lt

    @pl.when(n == 0)
    def _():
        _prenorm_to_scratch(h_scr, x_ref, gpre_ref, sb, lt)
        acc_scr[...] = jnp.zeros_like(acc_scr)

    h = h_scr[...]
    a_out = _dot(acta_ref[...].reshape(m, -1), wco_ref[...])
    b_out = _dot(actb_ref[...].reshape(m, -1), wso_ref[...])
    gate_a = jax.nn.sigmoid(_dot(h, wga_ref[...]))
    gate_b = jax.nn.sigmoid(_dot(h, wgb_ref[...]))
    merged = (gate_a * a_out + gate_b * b_out).astype(BF16)
    acc_scr[...] += _dot(merged, wo_ref[...])

    @pl.when(n == nn - 1)
    def _():
        _postnorm_residual(out_ref, x_ref, acc_scr, gpost_ref, sb, lt)


def _mix_out(x, act_a, act_b, w, *, sb, lt, tn, emit):
    ns, seq, d_model = x.shape
    d_conf = act_a.shape[-1]
    nn = d_model // tn
    nt = seq // lt
    m = sb * lt
    grid = (ns // sb, nt, nn)
    assert not emit or grid[:2] == (1, 1)
    w_names = ("w_co", "w_so", "w_ga", "w_gb")
    w_rows = (d_conf, d_conf, d_model, d_model)
    w_arrs, w_specs = zip(*[_col_tiled(w, name, rows, tn) for name, rows in zip(w_names, w_rows)])
    row3 = lambda s, t, n: (s, t, 0)
    fixed2 = lambda s, t, n: (0, 0)
    col2 = lambda s, t, n: (0, n)
    rowtile2 = lambda s, t, n: (n, 0)
    kern = functools.partial(_mix_out_kernel, sb=sb, lt=lt, nn=nn, emit=emit)
    res = pl.pallas_call(
        kern,
        grid=grid,
        in_specs=[
            pl.BlockSpec((sb, lt, d_model), row3),
            pl.BlockSpec((sb, lt, d_conf), row3),
            pl.BlockSpec((sb, lt, d_conf), row3),
            pl.BlockSpec((1, d_model), fixed2),
            *w_specs,
            pl.BlockSpec((tn, d_model), rowtile2),
            pl.BlockSpec((1, d_model), fixed2),
        ],
        out_specs=[pl.BlockSpec((sb, lt, d_model), row3)]
        + ([pl.BlockSpec((rows, tn), col2) for rows in w_rows]
           + [pl.BlockSpec((tn, d_model), rowtile2)] if emit else []),
        out_shape=[jax.ShapeDtypeStruct(x.shape, F32)]
        + ([jax.ShapeDtypeStruct((rows, d_model), BF16) for rows in w_rows]
           + [jax.ShapeDtypeStruct((d_model, d_model), BF16)] if emit else []),
        scratch_shapes=[
            pltpu.VMEM((m, d_model), BF16),
            pltpu.VMEM((m, d_model), F32),
        ],
        compiler_params=pltpu.CompilerParams(
            dimension_semantics=("arbitrary", "arbitrary", "arbitrary"),
            vmem_limit_bytes=VMEM_LIMIT_BYTES),
        name="mix_out",
    )(x, act_a, act_b, w["g_pre_mix"], *w_arrs, w["w_o"], w["g_post_mix"])
    w_bf16 = {name: (arr, 0) for name, arr in zip(w_names, res[1:1 + len(w_names)])}
    if emit:
        w_bf16["w_o"] = res[-1]
    return res[0], w_bf16


def _ffn_kernel(x_ref, histg_ref, histv_ref, gpre_ref, wug_ref, wuv_ref, cwg_ref, cwv_ref,
                wd_ref, gpost_ref, out_ref, newg_ref, newv_ref, *rest, sb, lt, nj, wf, emit):
    if emit:
        wug_ref, wuv_ref, wd_ref = _bf16_weight_tiles((wug_ref, wuv_ref, wd_ref), rest[:3])
        rest = rest[3:]
    h_scr, acc_scr, carry_scr, xpg_scr, xpv_scr = rest
    t = pl.program_id(1)
    j = pl.program_id(2)
    m = sb * lt
    tj = wug_ref.shape[-1]
    pad = xpg_scr.shape[1] - lt

    @pl.when(j == 0)
    def _():
        _prenorm_to_scratch(h_scr, x_ref, gpre_ref, sb, lt)
        acc_scr[...] = jnp.zeros_like(acc_scr)

    @pl.when(t == 0)
    def _():
        for slot, hist_ref in ((j, histg_ref), (nj + j, histv_ref)):
            carry_scr[slot] = jnp.zeros(carry_scr.shape[1:], F32)
            carry_scr[slot, :, pad - (wf - 1):pad, :] = hist_ref[...]

    h = h_scr[...]

    def conv(up, cw_ref, new_ref, slot, xp_scr):
        xp_scr[:, 0:pad, :] = carry_scr[slot]
        xp_scr[:, pad:pad + lt, :] = up.reshape(sb, lt, tj)
        y = None
        for k in range(wf):
            lo = pad - (wf - 1) + k
            term = cw_ref[k:k + 1, :] * xp_scr[:, lo:lo + lt, :]
            y = term if y is None else y + term
        carry_scr[slot] = xp_scr[:, lt:lt + pad, :]
        new_ref[...] = xp_scr[:, pad + lt - (wf - 1):pad + lt, :]
        return y.reshape(m, tj)

    f_gate = conv(_dot(h, wug_ref[...]), cwg_ref, newg_ref, j, xpg_scr)
    f_val = conv(_dot(h, wuv_ref[...]), cwv_ref, newv_ref, nj + j, xpv_scr)
    act = (f_gate * jax.nn.sigmoid(f_gate) * f_val).astype(BF16)
    acc_scr[...] += _dot(act, wd_ref[...])

    @pl.when(j == nj - 1)
    def _():
        _postnorm_residual(out_ref, x_ref, acc_scr, gpost_ref, sb, lt)


def _ffn(x, hist_f, w, *, sb, lt, tj, emit):
    ns, seq, d_model = x.shape
    d_ff = hist_f.shape[-1] // 2
    wf = hist_f.shape[1] + 1
    nj = d_ff // tj
    nt = seq // lt
    m = sb * lt
    pad = _round_up(wf - 1, SUBLANES)
    assert lt >= pad
    grid = (ns // sb, nt, nj)
    assert not emit or grid[:2] == (1, 1)
    (w_ug, spec_ug), (w_uv, spec_uv) = (_col_tiled(w, name, d_model, tj) for name in ("w_ug", "w_uv"))
    row3 = lambda s, t, j: (s, t, 0)
    fixed2 = lambda s, t, j: (0, 0)
    gate2 = lambda s, t, j: (0, j)
    val2 = lambda s, t, j: (0, nj + j)
    rowtile2 = lambda s, t, j: (j, 0)
    gate3 = lambda s, t, j: (s, 0, j)
    val3 = lambda s, t, j: (s, 0, nj + j)
    new3 = lambda s, t, j: (s, 0, jnp.where(t == nt - 1, j, 0))
    kern = functools.partial(_ffn_kernel, sb=sb, lt=lt, nj=nj, wf=wf, emit=emit)
    res = pl.pallas_call(
        kern,
        grid=grid,
        in_specs=[
            pl.BlockSpec((sb, lt, d_model), row3),
            pl.BlockSpec((sb, wf - 1, tj), gate3),
            pl.BlockSpec((sb, wf - 1, tj), val3),
            pl.BlockSpec((1, d_model), fixed2),
            spec_ug,
            spec_uv,
            pl.BlockSpec((wf, tj), gate2),
            pl.BlockSpec((wf, tj), val2),
            pl.BlockSpec((tj, d_model), rowtile2),
            pl.BlockSpec((1, d_model), fixed2),
        ],
        out_specs=[
            pl.BlockSpec((sb, lt, d_model), row3),
            pl.BlockSpec((sb, wf - 1, tj), new3),
            pl.BlockSpec((sb, wf - 1, tj), new3),
        ] + ([pl.BlockSpec((d_model, tj), gate2)] * 2 + [pl.BlockSpec((tj, d_model), rowtile2)]
             if emit else []),
        out_shape=[
            jax.ShapeDtypeStruct(x.shape, F32),
            jax.ShapeDtypeStruct((ns, wf - 1, d_ff), F32),
            jax.ShapeDtypeStruct((ns, wf - 1, d_ff), F32),
        ] + ([jax.ShapeDtypeStruct((d_model, d_ff), BF16)] * 2
             + [jax.ShapeDtypeStruct((d_ff, d_model), BF16)] if emit else []),
        scratch_shapes=[
            pltpu.VMEM((m, d_model), BF16),
            pltpu.VMEM((m, d_model), F32),
            pltpu.VMEM((2 * nj, sb, pad, tj), F32),
            pltpu.VMEM((sb, pad + lt, tj), F32),
            pltpu.VMEM((sb, pad + lt, tj), F32),
        ],
        compiler_params=pltpu.CompilerParams(
            dimension_semantics=("arbitrary", "arbitrary", "arbitrary"),
            vmem_limit_bytes=VMEM_LIMIT_BYTES),
        name="ffn",
    )(x, hist_f, hist_f, w["g_pre_ffn"], w_ug, w_uv, w["ffn_conv_w"], w["ffn_conv_w"],
      w["w_d"], w["g_post_ffn"])
    w_bf16 = {"w_ug": (res[3], 0), "w_uv": (res[4], 0), "w_d": res[5]} if emit else {}
    return res[0], jnp.concatenate([res[1], res[2]], axis=-1), w_bf16


def _layer(x, hist_a, hist_b, hist_f, w, *, sb, lt, tiles, lead_pad=0, emit=False):
    def zero_pad(v):
        return v.at[-1, :lead_pad].set(0.0) if lead_pad else v

    tc, tn, tj = tiles
    (act_a, act_b, new_a, new_b), w_in_bf16 = _mix_in(x, hist_a, hist_b, w, sb=sb, lt=lt, tc=tc, emit=emit)
    x1, w_out_bf16 = _mix_out(x, act_a, act_b, w, sb=sb, lt=lt, tn=tn, emit=emit)
    y, new_f, w_ffn_bf16 = _ffn(zero_pad(x1), hist_f, w, sb=sb, lt=lt, tj=tj, emit=emit)
    return zero_pad(y), new_a, new_b, new_f, {**w_in_bf16, **w_out_bf16, **w_ffn_bf16}


def kernel(x_prompt, x_sample, state_conf_conv, state_sconv, state_ffn_conv, meta_tokens, g_pre_mix, w_in, conf_conv_w, conf_conv_b, conf_ln_g, conf_ln_b, w_conf_out, sconv_w, w_sconv_out, w_o, g_post_mix, g_pre_ffn, w_up, ffn_conv_w, w_down, g_post_ffn):
    depth = w_in.shape[0]
    batch, seq, d_model = x_prompt.shape
    dec_batch, dec_seq, _ = x_sample.shape
    n_meta = meta_tokens.shape[0]
    d_conf, d_ff = w_conf_out.shape[1], w_down.shape[1]
    prompt_tile = 512
    assert seq % prompt_tile == 0

    assert n_meta <= dec_seq
    lead_pad = dec_seq - n_meta
    xm = jnp.pad(meta_tokens.astype(x_prompt.dtype), ((lead_pad, 0), (0, 0)))[None]
    xp = x_prompt
    xs = jnp.concatenate([x_sample, xm], axis=0)
    outs = [[] for _ in range(6)]
    for l in range(depth):
        small = {
            "g_pre_mix": g_pre_mix[l][None],
            "conf_conv_w": jnp.broadcast_to(conf_conv_w[l][:, None, :],
                                            (conf_conv_w.shape[1], SUBLANES, d_conf)),
            "conf_conv_b": conf_conv_b[l][None],
            "conf_ln_g": conf_ln_g[l][None], "conf_ln_b": conf_ln_b[l][None],
            "sconv_w": sconv_w[l], "g_post_mix": g_post_mix[l][None],
            "g_pre_ffn": g_pre_ffn[l][None], "ffn_conv_w": ffn_conv_w[l],
            "g_post_ffn": g_post_ffn[l][None],
        }
        w_f32 = {
            "w_val": (w_in[l], 0), "w_gate": (w_in[l], d_conf), "w_b": (w_in[l], 2 * d_conf),
            "w_c": (w_in[l], 3 * d_conf), "w_x": (w_in[l], 4 * d_conf),
            "w_ga": (w_in[l], 5 * d_conf), "w_gb": (w_in[l], 5 * d_conf + d_model),
            "w_co": (w_conf_out[l], 0), "w_so": (w_sconv_out[l], 0), "w_o": w_o[l],
            "w_ug": (w_up[l], 0), "w_uv": (w_up[l], d_ff), "w_d": w_down[l],
        }
        with_meta = lambda st: jnp.pad(st[l], ((0, 1), (0, 0), (0, 0)))
        xs, sa, sb_, sf, w_bf16 = _layer(
            xs, with_meta(state_conf_conv), with_meta(state_sconv), with_meta(state_ffn_conv),
            {**small, **w_f32}, sb=dec_batch + 1, lt=dec_seq, tiles=(128, 256, 256),
            lead_pad=lead_pad, emit=True)
        bcast = lambda hist: jnp.broadcast_to(hist[dec_batch:], (batch,) + hist.shape[1:])
        xp, pa, pb, pf, _ = _layer(xp, bcast(sa), bcast(sb_), bcast(sf), {**small, **w_bf16},
                                   sb=1, lt=prompt_tile, tiles=(256, 512, 512))
        for acc, val in zip(outs, (pa, pb, pf, sa[:dec_batch], sb_[:dec_batch], sf[:dec_batch])):
            acc.append(val)
    return (xp, xs[:dec_batch]) + tuple(jnp.stack(o) for o in outs)
```

```python
import functools

import jax
import jax.numpy as jnp
from jax import lax
from jax.experimental import pallas as pl
from jax.experimental.pallas import tpu as pltpu

EPS = 1e-6
SUBLANES = 8
CONV_ACC_ELEMS = 8 * 1024
MIX_IN_CHUNK_ROWS = 256
NORM_CHUNK_ROWS = 16
V7X_VMEM_BYTES = 64 * 1024 * 1024
VMEM_LIMIT_BYTES = V7X_VMEM_BYTES - 8 * 1024 * 1024

F32 = jnp.float32
BF16 = jnp.bfloat16


def _round_up(n, m):
    return (n + m - 1) // m * m


def _rmsnorm_rows(xf, g):
    ms = jnp.mean(xf * xf, axis=-1, keepdims=True)
    return xf * lax.rsqrt(ms + EPS) * g


def _prenorm_to_scratch(h_scr, x_ref, g_ref, sb, lt):
    for s0, s1, r0, r1 in _row_chunks(sb, lt, NORM_CHUNK_ROWS):
        rows = (s1 - s0) * (r1 - r0)
        xf = x_ref[s0:s1, r0:r1, :].reshape(rows, x_ref.shape[-1])
        row0 = s0 * lt + r0
        h_scr[row0:row0 + rows, :] = _rmsnorm_rows(xf, g_ref[...]).astype(BF16)


def _postnorm_residual(out_ref, x_ref, acc_scr, g_ref, sb, lt):
    for s0, s1, r0, r1 in _row_chunks(sb, lt, NORM_CHUNK_ROWS):
        rows = (s1 - s0) * (r1 - r0)
        row0 = s0 * lt + r0
        y = _rmsnorm_rows(acc_scr[row0:row0 + rows, :], g_ref[...])
        out_ref[s0:s1, r0:r1, :] = x_ref[s0:s1, r0:r1, :] + y.reshape(s1 - s0, r1 - r0, y.shape[-1])


def _dot(a, b):
    return jnp.dot(a, b, preferred_element_type=F32)


def _bf16_weight_tiles(w_refs, wout_refs):
    for src, dst in zip(w_refs, wout_refs):
        dst[...] = src[...].astype(BF16)
    return wout_refs


def _col_tiled(w, name, rows, tile):
    arr, off = w[name]
    assert off % tile == 0
    blk = off // tile
    return arr, pl.BlockSpec((rows, tile), lambda s, t, i, b=blk: (0, b + i))


def _row_chunks(sb, lt, rows):
    if lt >= rows:
        assert lt % rows == 0
        return [(s, s + 1, r0, r0 + rows) for s in range(sb) for r0 in range(0, lt, rows)]
    per = max(1, rows // lt)
    return [(s0, min(s0 + per, sb), 0, lt) for s0 in range(0, sb, per)]


def _mix_in_kernel(x_ref, hista_ref, histb_ref, gpre_ref, wv_ref, wg_ref, wb_ref, wc_ref, wx_ref,
                   cw_ref, cb_ref, lng_ref, lnb_ref, sw_ref,
                   acta_ref, actb_ref, newa_ref, newb_ref, *rest, sb, lt, nc, tc, wa, wu, emit):
    if emit:
        wv_ref, wg_ref, wb_ref, wc_ref, wx_ref = _bf16_weight_tiles(
            (wv_ref, wg_ref, wb_ref, wc_ref, wx_ref), rest[:5])
        rest = rest[5:]
    h_scr, xpa_scr, xpu_scr, aconv_scr, ph_scr = rest
    t = pl.program_id(1)
    c = pl.program_id(2)
    pad_a = xpa_scr.shape[2] - lt
    pad_u = xpu_scr.shape[2] - lt

    @pl.when(c == 0)
    def _():
        _prenorm_to_scratch(h_scr, x_ref, gpre_ref, sb, lt)

    @pl.when((t == 0) & (c == 0))
    def _():
        for cc in range(nc):
            xpa_scr[cc, :, pad_a - (wa - 1):pad_a, :] = hista_ref[:, :, cc * tc:(cc + 1) * tc]
            xpu_scr[cc, :, pad_u - (wu - 1):pad_u, :] = histb_ref[:, :, cc * tc:(cc + 1) * tc]

    rc = min(lt, max(SUBLANES, CONV_ACC_ELEMS // tc))
    for s0, s1, r0, r1 in _row_chunks(sb, lt, MIX_IN_CHUNK_ROWS):
        nr = r1 - r0
        row0 = s0 * lt + r0
        hc = h_scr[row0:row0 + (s1 - s0) * nr, :]
        shape3 = (s1 - s0, nr, tc)
        a = _dot(hc, wv_ref[...]) * jax.nn.sigmoid(_dot(hc, wg_ref[...]))
        xpa_scr[c, s0:s1, pad_a + r0:pad_a + r1, :] = a.reshape(shape3)
        lo = 0 if r0 == 0 else r0 + pad_a - SUBLANES
        hi = r1 + pad_a - SUBLANES
        for p in range(1, SUBLANES):
            ph_scr[p - 1, s0:s1, lo:hi, :] = xpa_scr[c, s0:s1, lo + p:hi + p, :]
        for s in range(s0, s1):
            for q0 in range(r0, r1, rc):
                acc = None
                for k in range(wa):
                    q, p = divmod(pad_a - (wa - 1) + k, SUBLANES)
                    at = q * SUBLANES + q0
                    if p == 0:
                        src = xpa_scr[c, s, at:at + rc, :]
                    else:
                        src = ph_scr[p - 1, s, at:at + rc, :]
                    term = cw_ref[k][None] * src.reshape(rc // SUBLANES, SUBLANES, tc)
                    acc = term if acc is None else acc + term
                aconv_scr[c, s, q0:q0 + rc, :] = acc.reshape(rc, tc) + cb_ref[...]
        u = _dot(hc, wc_ref[...]) * _dot(hc, wx_ref[...])
        xpu_scr[c, s0:s1, pad_u + r0:pad_u + r1, :] = u.reshape(shape3)
        uconv = None
        for k in range(wu):
            at = pad_u - (wu - 1) + k + r0
            term = sw_ref[k:k + 1, :] * xpu_scr[c, s0:s1, at:at + nr, :]
            uconv = term if uconv is None else uconv + term
        s_b = _dot(hc, wb_ref[...]).reshape(shape3)
        actb_ref[s0:s1, r0:r1, :] = (s_b * uconv).astype(BF16)
    xpa_scr[c, :, pad_a - (wa - 1):pad_a, :] = xpa_scr[c, :, pad_a + lt - (wa - 1):pad_a + lt, :]
    xpu_scr[c, :, pad_u - (wu - 1):pad_u, :] = xpu_scr[c, :, pad_u + lt - (wu - 1):pad_u + lt, :]

    @pl.when(c == nc - 1)
    def _():
        d_conf = nc * tc
        tot = None
        for cc in range(nc):
            part = jnp.sum(aconv_scr[cc], axis=-1, keepdims=True)
            tot = part if tot is None else tot + part
        mu = tot / d_conf
        sq = None
        for cc in range(nc):
            xc = aconv_scr[cc] - mu
            part = jnp.sum(xc * xc, axis=-1, keepdims=True)
            sq = part if sq is None else sq + part
        rstd = lax.rsqrt(sq / d_conf + EPS)
        for cc in range(nc):
            sl = slice(cc * tc, (cc + 1) * tc)
            y = (aconv_scr[cc] - mu) * rstd * lng_ref[:, sl] + lnb_ref[:, sl]
            acta_ref[:, :, sl] = (y * jax.nn.sigmoid(y)).astype(BF16)
            newa_ref[:, :, sl] = xpa_scr[cc, :, pad_a - (wa - 1):pad_a, :]
            newb_ref[:, :, sl] = xpu_scr[cc, :, pad_u - (wu - 1):pad_u, :]


def _mix_in(x, hist_a, hist_b, w, *, sb, lt, tc, emit):
    ns, seq, d_model = x.shape
    d_conf = hist_a.shape[-1]
    wa = hist_a.shape[1] + 1
    wu = hist_b.shape[1] + 1
    nc = d_conf // tc
    nt = seq // lt
    m = sb * lt
    pad_a = _round_up(wa - 1, SUBLANES)
    pad_u = _round_up(wu - 1, SUBLANES)
    grid = (ns // sb, nt, nc)
    assert not emit or grid[:2] == (1, 1)
    w_names = ("w_val", "w_gate", "w_b", "w_c", "w_x")
    w_arrs, w_specs = zip(*[_col_tiled(w, name, d_model, tc) for name in w_names])
    row3 = lambda s, t, c: (s, t, 0)
    stream3 = lambda s, t, c: (s, 0, 0)
    chan2 = lambda s, t, c: (0, c)
    fixed2 = lambda s, t, c: (0, 0)
    kern = functools.partial(_mix_in_kernel, sb=sb, lt=lt, nc=nc, tc=tc, wa=wa, wu=wu, emit=emit)
    n_emit = len(w_names) if emit else 0
    res = pl.pallas_call(
        kern,
        grid=grid,
        in_specs=[
            pl.BlockSpec((sb, lt, d_model), row3),
            pl.BlockSpec((sb, wa - 1, d_conf), stream3),
            pl.BlockSpec((sb, wu - 1, d_conf), stream3),
            pl.BlockSpec((1, d_model), fixed2),
            *w_specs,
            pl.BlockSpec((wa, SUBLANES, tc), lambda s, t, c: (0, 0, c)),
            pl.BlockSpec((1, tc), chan2),
            pl.BlockSpec((1, d_conf), fixed2),
            pl.BlockSpec((1, d_conf), fixed2),
            pl.BlockSpec((wu, tc), chan2),
        ],
        out_specs=[
            pl.BlockSpec((sb, lt, d_conf), row3),
            pl.BlockSpec((sb, lt, tc), lambda s, t, c: (s, t, c)),
            pl.BlockSpec((sb, wa - 1, d_conf), stream3),
            pl.BlockSpec((sb, wu - 1, d_conf), stream3),
        ] + [pl.BlockSpec((d_model, tc), chan2)] * n_emit,
        out_shape=[
            jax.ShapeDtypeStruct((ns, seq, d_conf), BF16),
            jax.ShapeDtypeStruct((ns, seq, d_conf), BF16),
            jax.ShapeDtypeStruct(hist_a.shape, F32),
            jax.ShapeDtypeStruct(hist_b.shape, F32),
        ] + [jax.ShapeDtypeStruct((d_model, d_conf), BF16)] * n_emit,
        scratch_shapes=[
            pltpu.VMEM((m, d_model), BF16),
            pltpu.VMEM((nc, sb, pad_a + lt, tc), F32),
            pltpu.VMEM((nc, sb, pad_u + lt, tc), F32),
            pltpu.VMEM((nc, sb, lt, tc), F32),
            pltpu.VMEM((SUBLANES - 1, sb, lt + pad_a - SUBLANES, tc), F32),
        ],
        compiler_params=pltpu.CompilerParams(
            dimension_semantics=("arbitrary", "arbitrary", "arbitrary"),
            vmem_limit_bytes=VMEM_LIMIT_BYTES),
        name="mix_in",
    )(x, hist_a, hist_b, w["g_pre_mix"], *w_arrs,
      w["conf_conv_w"], w["conf_conv_b"], w["conf_ln_g"], w["conf_ln_b"], w["sconv_w"])
    return res[:4], {name: (arr, 0) for name, arr in zip(w_names, res[4:])}


def _mix_out_kernel(x_ref, acta_ref, actb_ref, gpre_ref, wco_ref, wso_ref, wga_ref, wgb_ref,
                    wo_ref, gpost_ref, out_ref, *rest, sb, lt, nn, emit):
    if emit:
        wco_ref, wso_ref, wga_ref, wgb_ref, wo_ref = _bf16_weight_tiles(
            (wco_ref, wso_ref, wga_ref, wgb_ref, wo_ref), rest[:5])
        rest = rest[5:]
    h_scr, acc_scr = rest
    n = pl.program_id(2)
    m = sb * lt

    @pl.when(n == 0)
    def _():
        _prenorm_to_scratch(h_scr, x_ref, gpre_ref, sb, lt)
        acc_scr[...] = jnp.zeros_like(acc_scr)

    h = h_scr[...]
    a_out = _dot(acta_ref[...].reshape(m, -1), wco_ref[...])
    b_out = _dot(actb_ref[...].reshape(m, -1), wso_ref[...])
    gate_a = jax.nn.sigmoid(_dot(h, wga_ref[...]))
    gate_b = jax.nn.sigmoid(_dot(h, wgb_ref[...]))
    merged = (gate_a * a_out + gate_b * b_out).astype(BF16)
    acc_scr[...] += _dot(merged, wo_ref[...])

    @pl.when(n == nn - 1)
    def _():
        _postnorm_residual(out_ref, x_ref, acc_scr, gpost_ref, sb, lt)


def _mix_out(x, act_a, act_b, w, *, sb, lt, tn, emit):
    ns, seq, d_model = x.shape
    d_conf = act_a.shape[-1]
    nn = d_model // tn
    nt = seq // lt
    m = sb * lt
    grid = (ns // sb, nt, nn)
    assert not emit or grid[:2] == (1, 1)
    w_names = ("w_co", "w_so", "w_ga", "w_gb")
    w_rows = (d_conf, d_conf, d_model, d_model)
    w_arrs, w_specs = zip(*[_col_tiled(w, name, rows, tn) for name, rows in zip(w_names, w_rows)])
    row3 = lambda s, t, n: (s, t, 0)
    fixed2 = lambda s, t, n: (0, 0)
    col2 = lambda s, t, n: (0, n)
    rowtile2 = lambda s, t, n: (n, 0)
    kern = functools.partial(_mix_out_kernel, sb=sb, lt=lt, nn=nn, emit=emit)
    res = pl.pallas_call(
        kern,
        grid=grid,
        in_specs=[
            pl.BlockSpec((sb, lt, d_model), row3),
            pl.BlockSpec((sb, lt, d_conf), row3),
            pl.BlockSpec((sb, lt, d_conf), row3),
            pl.BlockSpec((1, d_model), fixed2),
            *w_specs,
            pl.BlockSpec((tn, d_model), rowtile2),
            pl.BlockSpec((1, d_model), fixed2),
        ],
        out_specs=[pl.BlockSpec((sb, lt, d_model), row3)]
        + ([pl.BlockSpec((rows, tn), col2) for rows in w_rows]
           + [pl.BlockSpec((tn, d_model), rowtile2)] if emit else []),
        out_shape=[jax.ShapeDtypeStruct(x.shape, F32)]
        + ([jax.ShapeDtypeStruct((rows, d_model), BF16) for rows in w_rows]
           + [jax.ShapeDtypeStruct((d_model, d_model), BF16)] if emit else []),
        scratch_shapes=[
            pltpu.VMEM((m, d_model), BF16),
            pltpu.VMEM((m, d_model), F32),
        ],
        compiler_params=pltpu.CompilerParams(
            dimension_semantics=("arbitrary", "arbitrary", "arbitrary"),
            vmem_limit_bytes=VMEM_LIMIT_BYTES),
        name="mix_out",
    )(x, act_a, act_b, w["g_pre_mix"], *w_arrs, w["w_o"], w["g_post_mix"])
    w_bf16 = {name: (arr, 0) for name, arr in zip(w_names, res[1:1 + len(w_names)])}
    if emit:
        w_bf16["w_o"] = res[-1]
    return res[0], w_bf16


def _ffn_kernel(x_ref, histg_ref, histv_ref, gpre_ref, wug_ref, wuv_ref, cwg_ref, cwv_ref,
                wd_ref, gpost_ref, out_ref, newg_ref, newv_ref, *rest, sb, lt, nj, wf, emit):
    if emit:
        wug_ref, wuv_ref, wd_ref = _bf16_weight_tiles((wug_ref, wuv_ref, wd_ref), rest[:3])
        rest = rest[3:]
    h_scr, acc_scr, carry_scr, xpg_scr, xpv_scr = rest
    t = pl.program_id(1)
    j = pl.program_id(2)
    m = sb * lt
    tj = wug_ref.shape[-1]
    pad = xpg_scr.shape[1] - lt

    @pl.when(j == 0)
    def _():
        _prenorm_to_scratch(h_scr, x_ref, gpre_ref, sb, lt)
        acc_scr[...] = jnp.zeros_like(acc_scr)

    @pl.when(t == 0)
    def _():
        for slot, hist_ref in ((j, histg_ref), (nj + j, histv_ref)):
            carry_scr[slot] = jnp.zeros(carry_scr.shape[1:], F32)
            carry_scr[slot, :, pad - (wf - 1):pad, :] = hist_ref[...]

    h = h_scr[...]

    def conv(up, cw_ref, new_ref, slot, xp_scr):
        xp_scr[:, 0:pad, :] = carry_scr[slot]
        xp_scr[:, pad:pad + lt, :] = up.reshape(sb, lt, tj)
        y = None
        for k in range(wf):
            lo = pad - (wf - 1) + k
            term = cw_ref[k:k + 1, :] * xp_scr[:, lo:lo + lt, :]
            y = term if y is None else y + term
        carry_scr[slot] = xp_scr[:, lt:lt + pad, :]
        new_ref[...] = xp_scr[:, pad + lt - (wf - 1):pad + lt, :]
        return y.reshape(m, tj)

    f_gate = conv(_dot(h, wug_ref[...]), cwg_ref, newg_ref, j, xpg_scr)
    f_val = conv(_dot(h, wuv_ref[...]), cwv_ref, newv_ref, nj + j, xpv_scr)
    act = (f_gate * jax.nn.sigmoid(f_gate) * f_val).astype(BF16)
    acc_scr[...] += _dot(act, wd_ref[...])

    @pl.when(j == nj - 1)
    def _():
        _postnorm_residual(out_ref, x_ref, acc_scr, gpost_ref, sb, lt)


def _ffn(x, hist_f, w, *, sb, lt, tj, emit, io_buffers=2):
    ns, seq, d_model = x.shape
    d_ff = hist_f.shape[-1] // 2
    wf = hist_f.shape[1] + 1
    nj = d_ff // tj
    nt = seq // lt
    m = sb * lt
    pad = _round_up(wf - 1, SUBLANES)
    assert lt >= pad
    grid = (ns // sb, nt, nj)
    assert not emit or grid[:2] == (1, 1)
    (w_ug, spec_ug), (w_uv, spec_uv) = (_col_tiled(w, name, d_model, tj) for name in ("w_ug", "w_uv"))
    row3 = lambda s, t, j: (s, t, 0)
    fixed2 = lambda s, t, j: (0, 0)
    gate2 = lambda s, t, j: (0, j)
    val2 = lambda s, t, j: (0, nj + j)
    rowtile2 = lambda s, t, j: (j, 0)
    gate3 = lambda s, t, j: (s, 0, j)
    val3 = lambda s, t, j: (s, 0, nj + j)
    new3 = lambda s, t, j: (s, 0, jnp.where(t == nt - 1, j, 0))
    io_mode = pl.Buffered(io_buffers)
    kern = functools.partial(_ffn_kernel, sb=sb, lt=lt, nj=nj, wf=wf, emit=emit)
    res = pl.pallas_call(
        kern,
        grid=grid,
        in_specs=[
            pl.BlockSpec((sb, lt, d_model), row3, pipeline_mode=io_mode),
            pl.BlockSpec((sb, wf - 1, tj), gate3),
            pl.BlockSpec((sb, wf - 1, tj), val3),
            pl.BlockSpec((1, d_model), fixed2),
            spec_ug,
            spec_uv,
            pl.BlockSpec((wf, tj), gate2),
            pl.BlockSpec((wf, tj), val2),
            pl.BlockSpec((tj, d_model), rowtile2),
            pl.BlockSpec((1, d_model), fixed2),
        ],
        out_specs=[
            pl.BlockSpec((sb, lt, d_model), row3, pipeline_mode=io_mode),
            pl.BlockSpec((sb, wf - 1, tj), new3),
            pl.BlockSpec((sb, wf - 1, tj), new3),
        ] + ([pl.BlockSpec((d_model, tj), gate2)] * 2 + [pl.BlockSpec((tj, d_model), rowtile2)]
             if emit else []),
        out_shape=[
            jax.ShapeDtypeStruct(x.shape, F32),
            jax.ShapeDtypeStruct((ns, wf - 1, d_ff), F32),
            jax.ShapeDtypeStruct((ns, wf - 1, d_ff), F32),
        ] + ([jax.ShapeDtypeStruct((d_model, d_ff), BF16)] * 2
             + [jax.ShapeDtypeStruct((d_ff, d_model), BF16)] if emit else []),
        scratch_shapes=[
            pltpu.VMEM((m, d_model), BF16),
            pltpu.VMEM((m, d_model), F32),
            pltpu.VMEM((2 * nj, sb, pad, tj), F32),
            pltpu.VMEM((sb, pad + lt, tj), F32),
            pltpu.VMEM((sb, pad + lt, tj), F32),
        ],
        compiler_params=pltpu.CompilerParams(
            dimension_semantics=("arbitrary", "arbitrary", "arbitrary"),
            vmem_limit_bytes=VMEM_LIMIT_BYTES),
        name="ffn",
    )(x, hist_f, hist_f, w["g_pre_ffn"], w_ug, w_uv, w["ffn_conv_w"], w["ffn_conv_w"],
      w["w_d"], w["g_post_ffn"])
    w_bf16 = {"w_ug": (res[3], 0), "w_uv": (res[4], 0), "w_d": res[5]} if emit else {}
    return res[0], jnp.concatenate([res[1], res[2]], axis=-1), w_bf16


def _layer(x, hist_a, hist_b, hist_f, w, *, sb, lt, tiles, ffn_rows=None, lead_pad=0, emit=False):
    def zero_pad(v):
        return v.at[-1, :lead_pad].set(0.0) if lead_pad else v

    tc, tn, tj = tiles
    (act_a, act_b, new_a, new_b), w_in_bf16 = _mix_in(x, hist_a, hist_b, w, sb=sb, lt=lt, tc=tc, emit=emit)
    x1, w_out_bf16 = _mix_out(x, act_a, act_b, w, sb=sb, lt=lt, tn=tn, emit=emit)
    y, new_f, w_ffn_bf16 = _ffn(zero_pad(x1), hist_f, w, sb=sb, lt=ffn_rows or lt, tj=tj, emit=emit,
                                io_buffers=1 if ffn_rows else 2)
    return zero_pad(y), new_a, new_b, new_f, {**w_in_bf16, **w_out_bf16, **w_ffn_bf16}


def kernel(x_prompt, x_sample, state_conf_conv, state_sconv, state_ffn_conv, meta_tokens, g_pre_mix, w_in, conf_conv_w, conf_conv_b, conf_ln_g, conf_ln_b, w_conf_out, sconv_w, w_sconv_out, w_o, g_post_mix, g_pre_ffn, w_up, ffn_conv_w, w_down, g_post_ffn):
    depth = w_in.shape[0]
    batch, seq, d_model = x_prompt.shape
    dec_batch, dec_seq, _ = x_sample.shape
    n_meta = meta_tokens.shape[0]
    d_conf, d_ff = w_conf_out.shape[1], w_down.shape[1]
    prompt_tile = 512
    assert seq % prompt_tile == 0

    assert n_meta <= dec_seq
    lead_pad = dec_seq - n_meta
    xm = jnp.pad(meta_tokens.astype(x_prompt.dtype), ((lead_pad, 0), (0, 0)))[None]
    xp = x_prompt
    xs = jnp.concatenate([x_sample, xm], axis=0)
    outs = [[] for _ in range(6)]
    for l in range(depth):
        small = {
            "g_pre_mix": g_pre_mix[l][None],
            "conf_conv_w": jnp.broadcast_to(conf_conv_w[l][:, None, :],
                                            (conf_conv_w.shape[1], SUBLANES, d_conf)),
            "conf_conv_b": conf_conv_b[l][None],
            "conf_ln_g": conf_ln_g[l][None], "conf_ln_b": conf_ln_b[l][None],
            "sconv_w": sconv_w[l], "g_post_mix": g_post_mix[l][None],
            "g_pre_ffn": g_pre_ffn[l][None], "ffn_conv_w": ffn_conv_w[l],
            "g_post_ffn": g_post_ffn[l][None],
        }
        w_f32 = {
            "w_val": (w_in[l], 0), "w_gate": (w_in[l], d_conf), "w_b": (w_in[l], 2 * d_conf),
            "w_c": (w_in[l], 3 * d_conf), "w_x": (w_in[l], 4 * d_conf),
            "w_ga": (w_in[l], 5 * d_conf), "w_gb": (w_in[l], 5 * d_conf + d_model),
            "w_co": (w_conf_out[l], 0), "w_so": (w_sconv_out[l], 0), "w_o": w_o[l],
            "w_ug": (w_up[l], 0), "w_uv": (w_up[l], d_ff), "w_d": w_down[l],
        }
        with_meta = lambda st: jnp.pad(st[l], ((0, 1), (0, 0), (0, 0)))
        xs, sa, sb_, sf, w_bf16 = _layer(
            xs, with_meta(state_conf_conv), with_meta(state_sconv), with_meta(state_ffn_conv),
            {**small, **w_f32}, sb=dec_batch + 1, lt=dec_seq, tiles=(128, 256, 256),
            lead_pad=lead_pad, emit=True)
        bcast = lambda hist: jnp.broadcast_to(hist[dec_batch:], (batch,) + hist.shape[1:])
        xp, pa, pb, pf, _ = _layer(xp, bcast(sa), bcast(sb_), bcast(sf), {**small, **w_bf16},
                                   sb=1, lt=prompt_tile, tiles=(512, 512, 512), ffn_rows=2 * prompt_tile)
        for acc, val in zip(outs, (pa, pb, pf, sa[:dec_batch], sb_[:dec_batch], sf[:dec_batch])):
            acc.append(val)
    return (xp, xs[:dec_batch]) + tuple(jnp.stack(o) for o in outs)
```

```python
import functools

import jax
import jax.numpy as jnp
from jax import lax
from jax.experimental import pallas as pl
from jax.experimental.pallas import tpu as pltpu

EPS = 1e-6
SUBLANES = 8
CONV_ACC_ELEMS = 8 * 1024
MIX_IN_CHUNK_ROWS = 256
NORM_CHUNK_ROWS = 16
V7X_VMEM_BYTES = 64 * 1024 * 1024
VMEM_LIMIT_BYTES = V7X_VMEM_BYTES - 8 * 1024 * 1024

F32 = jnp.float32
BF16 = jnp.bfloat16


def _round_up(n, m):
    return (n + m - 1) // m * m


def _rmsnorm_rows(xf, g):
    ms = jnp.mean(xf * xf, axis=-1, keepdims=True)
    return xf * lax.rsqrt(ms + EPS) * g


def _prenorm(h_ref, x_ref, g_ref, sb, lt):
    for s0, s1, r0, r1 in _row_chunks(sb, lt, NORM_CHUNK_ROWS):
        shape3 = (s1 - s0, r1 - r0, x_ref.shape[-1])
        xf = x_ref[s0:s1, r0:r1, :].reshape(shape3[0] * shape3[1], shape3[2])
        h_ref[s0:s1, r0:r1, :] = _rmsnorm_rows(xf, g_ref[...]).astype(BF16).reshape(shape3)


def _postnorm_residual(out_ref, x_ref, acc_scr, g_ref, sb, lt):
    for s0, s1, r0, r1 in _row_chunks(sb, lt, NORM_CHUNK_ROWS):
        rows = (s1 - s0) * (r1 - r0)
        row0 = s0 * lt + r0
        y = _rmsnorm_rows(acc_scr[row0:row0 + rows, :], g_ref[...])
        out_ref[s0:s1, r0:r1, :] = x_ref[s0:s1, r0:r1, :] + y.reshape(s1 - s0, r1 - r0, y.shape[-1])


def _dot(a, b):
    return jnp.dot(a, b, preferred_element_type=F32)


def _bf16_weight_tiles(w_refs, wout_refs):
    for src, dst in zip(w_refs, wout_refs):
        dst[...] = src[...].astype(BF16)
    return wout_refs


def _col_tiled(w, name, rows, tile):
    arr, off = w[name]
    assert off % tile == 0
    blk = off // tile
    return arr, pl.BlockSpec((rows, tile), lambda s, t, i, b=blk: (0, b + i))


def _row_chunks(sb, lt, rows):
    if lt >= rows:
        assert lt % rows == 0
        return [(s, s + 1, r0, r0 + rows) for s in range(sb) for r0 in range(0, lt, rows)]
    per = max(1, rows // lt)
    return [(s0, min(s0 + per, sb), 0, lt) for s0 in range(0, sb, per)]


def _mix_in_kernel(x_ref, hista_ref, histb_ref, gpre_ref, wv_ref, wg_ref, wb_ref, wc_ref, wx_ref,
                   cw_ref, cb_ref, lng_ref, lnb_ref, sw_ref,
                   acta_ref, actb_ref, newa_ref, newb_ref, h_ref, *rest, sb, lt, nc, tc, wa, wu, emit):
    if emit:
        wv_ref, wg_ref, wb_ref, wc_ref, wx_ref = _bf16_weight_tiles(
            (wv_ref, wg_ref, wb_ref, wc_ref, wx_ref), rest[:5])
        rest = rest[5:]
    xpa_scr, xpu_scr, aconv_scr, ph_scr = rest
    t = pl.program_id(1)
    c = pl.program_id(2)
    pad_a = xpa_scr.shape[2] - lt
    pad_u = xpu_scr.shape[2] - lt

    @pl.when(c == 0)
    def _():
        _prenorm(h_ref, x_ref, gpre_ref, sb, lt)

    @pl.when((t == 0) & (c == 0))
    def _():
        for cc in range(nc):
            xpa_scr[cc, :, pad_a - (wa - 1):pad_a, :] = hista_ref[:, :, cc * tc:(cc + 1) * tc]
            xpu_scr[cc, :, pad_u - (wu - 1):pad_u, :] = histb_ref[:, :, cc * tc:(cc + 1) * tc]

    rc = min(lt, max(SUBLANES, CONV_ACC_ELEMS // tc))
    for s0, s1, r0, r1 in _row_chunks(sb, lt, MIX_IN_CHUNK_ROWS):
        nr = r1 - r0
        hc = h_ref[s0:s1, r0:r1, :].reshape((s1 - s0) * nr, h_ref.shape[-1])
        shape3 = (s1 - s0, nr, tc)
        a = _dot(hc, wv_ref[...]) * jax.nn.sigmoid(_dot(hc, wg_ref[...]))
        xpa_scr[c, s0:s1, pad_a + r0:pad_a + r1, :] = a.reshape(shape3)
        lo = 0 if r0 == 0 else r0 + pad_a - SUBLANES
        hi = r1 + pad_a - SUBLANES
        for p in range(1, SUBLANES):
            ph_scr[p - 1, s0:s1, lo:hi, :] = xpa_scr[c, s0:s1, lo + p:hi + p, :]
        for s in range(s0, s1):
            for q0 in range(r0, r1, rc):
                acc = None
                for k in range(wa):
                    q, p = divmod(pad_a - (wa - 1) + k, SUBLANES)
                    at = q * SUBLANES + q0
                    if p == 0:
                        src = xpa_scr[c, s, at:at + rc, :]
                    else:
                        src = ph_scr[p - 1, s, at:at + rc, :]
                    term = cw_ref[k][None] * src.reshape(rc // SUBLANES, SUBLANES, tc)
                    acc = term if acc is None else acc + term
                aconv_scr[c, s, q0:q0 + rc, :] = acc.reshape(rc, tc) + cb_ref[...]
        u = _dot(hc, wc_ref[...]) * _dot(hc, wx_ref[...])
        xpu_scr[c, s0:s1, pad_u + r0:pad_u + r1, :] = u.reshape(shape3)
        uconv = None
        for k in range(wu):
            at = pad_u - (wu - 1) + k + r0
            term = sw_ref[k:k + 1, :] * xpu_scr[c, s0:s1, at:at + nr, :]
            uconv = term if uconv is None else uconv + term
        s_b = _dot(hc, wb_ref[...]).reshape(shape3)
        actb_ref[s0:s1, r0:r1, :] = (s_b * uconv).astype(BF16)
    xpa_scr[c, :, pad_a - (wa - 1):pad_a, :] = xpa_scr[c, :, pad_a + lt - (wa - 1):pad_a + lt, :]
    xpu_scr[c, :, pad_u - (wu - 1):pad_u, :] = xpu_scr[c, :, pad_u + lt - (wu - 1):pad_u + lt, :]

    @pl.when(c == nc - 1)
    def _():
        d_conf = nc * tc
        tot = None
        for cc in range(nc):
            part = jnp.sum(aconv_scr[cc], axis=-1, keepdims=True)
            tot = part if tot is None else tot + part
        mu = tot / d_conf
        sq = None
        for cc in range(nc):
            xc = aconv_scr[cc] - mu
            part = jnp.sum(xc * xc, axis=-1, keepdims=True)
            sq = part if sq is None else sq + part
        rstd = lax.rsqrt(sq / d_conf + EPS)
        for cc in range(nc):
            sl = slice(cc * tc, (cc + 1) * tc)
            y = (aconv_scr[cc] - mu) * rstd * lng_ref[:, sl] + lnb_ref[:, sl]
            acta_ref[:, :, sl] = (y * jax.nn.sigmoid(y)).astype(BF16)
            newa_ref[:, :, sl] = xpa_scr[cc, :, pad_a - (wa - 1):pad_a, :]
            newb_ref[:, :, sl] = xpu_scr[cc, :, pad_u - (wu - 1):pad_u, :]


def _mix_in(x, hist_a, hist_b, w, *, sb, lt, tc, emit):
    ns, seq, d_model = x.shape
    d_conf = hist_a.shape[-1]
    wa = hist_a.shape[1] + 1
    wu = hist_b.shape[1] + 1
    nc = d_conf // tc
    nt = seq // lt
    pad_a = _round_up(wa - 1, SUBLANES)
    pad_u = _round_up(wu - 1, SUBLANES)
    grid = (ns // sb, nt, nc)
    assert not emit or grid[:2] == (1, 1)
    w_names = ("w_val", "w_gate", "w_b", "w_c", "w_x")
    w_arrs, w_specs = zip(*[_col_tiled(w, name, d_model, tc) for name in w_names])
    row3 = lambda s, t, c: (s, t, 0)
    stream3 = lambda s, t, c: (s, 0, 0)
    chan2 = lambda s, t, c: (0, c)
    fixed2 = lambda s, t, c: (0, 0)
    kern = functools.partial(_mix_in_kernel, sb=sb, lt=lt, nc=nc, tc=tc, wa=wa, wu=wu, emit=emit)
    n_emit = len(w_names) if emit else 0
    res = pl.pallas_call(
        kern,
        grid=grid,
        in_specs=[
            pl.BlockSpec((sb, lt, d_model), row3),
            pl.BlockSpec((sb, wa - 1, d_conf), stream3),
            pl.BlockSpec((sb, wu - 1, d_conf), stream3),
            pl.BlockSpec((1, d_model), fixed2),
            *w_specs,
            pl.BlockSpec((wa, SUBLANES, tc), lambda s, t, c: (0, 0, c)),
            pl.BlockSpec((1, tc), chan2),
            pl.BlockSpec((1, d_conf), fixed2),
            pl.BlockSpec((1, d_conf), fixed2),
            pl.BlockSpec((wu, tc), chan2),
        ],
        out_specs=[
            pl.BlockSpec((sb, lt, d_conf), row3),
            pl.BlockSpec((sb, lt, tc), lambda s, t, c: (s, t, c)),
            pl.BlockSpec((sb, wa - 1, d_conf), stream3),
            pl.BlockSpec((sb, wu - 1, d_conf), stream3),
            pl.BlockSpec((sb, lt, d_model), row3),
        ] + [pl.BlockSpec((d_model, tc), chan2)] * n_emit,
        out_shape=[
            jax.ShapeDtypeStruct((ns, seq, d_conf), BF16),
            jax.ShapeDtypeStruct((ns, seq, d_conf), BF16),
            jax.ShapeDtypeStruct(hist_a.shape, F32),
            jax.ShapeDtypeStruct(hist_b.shape, F32),
            jax.ShapeDtypeStruct(x.shape, BF16),
        ] + [jax.ShapeDtypeStruct((d_model, d_conf), BF16)] * n_emit,
        scratch_shapes=[
            pltpu.VMEM((nc, sb, pad_a + lt, tc), F32),
            pltpu.VMEM((nc, sb, pad_u + lt, tc), F32),
            pltpu.VMEM((nc, sb, lt, tc), F32),
            pltpu.VMEM((SUBLANES - 1, sb, lt + pad_a - SUBLANES, tc), F32),
        ],
        compiler_params=pltpu.CompilerParams(
            dimension_semantics=("arbitrary", "arbitrary", "arbitrary"),
            vmem_limit_bytes=VMEM_LIMIT_BYTES),
        name="mix_in",
    )(x, hist_a, hist_b, w["g_pre_mix"], *w_arrs,
      w["conf_conv_w"], w["conf_conv_b"], w["conf_ln_g"], w["conf_ln_b"], w["sconv_w"])
    return res[:5], {name: (arr, 0) for name, arr in zip(w_names, res[5:])}


def _mix_out_kernel(x_ref, h_ref, acta_ref, actb_ref, wco_ref, wso_ref, wga_ref, wgb_ref,
                    wo_ref, gpost_ref, out_ref, *rest, sb, lt, nn, emit):
    if emit:
        wco_ref, wso_ref, wga_ref, wgb_ref, wo_ref = _bf16_weight_tiles(
            (wco_ref, wso_ref, wga_ref, wgb_ref, wo_ref), rest[:5])
        rest = rest[5:]
    (acc_scr,) = rest
    n = pl.program_id(2)
    m = sb * lt

    @pl.when(n == 0)
    def _():
        acc_scr[...] = jnp.zeros_like(acc_scr)

    h = h_ref[...].reshape(m, -1)
    a_out = _dot(acta_ref[...].reshape(m, -1), wco_ref[...])
    b_out = _dot(actb_ref[...].reshape(m, -1), wso_ref[...])
    gate_a = jax.nn.sigmoid(_dot(h, wga_ref[...]))
    gate_b = jax.nn.sigmoid(_dot(h, wgb_ref[...]))
    merged = (gate_a * a_out + gate_b * b_out).astype(BF16)
    acc_scr[...] += _dot(merged, wo_ref[...])

    @pl.when(n == nn - 1)
    def _():
        _postnorm_residual(out_ref, x_ref, acc_scr, gpost_ref, sb, lt)


def _mix_out(x, h, act_a, act_b, w, *, sb, lt, tn, emit):
    ns, seq, d_model = x.shape
    d_conf = act_a.shape[-1]
    nn = d_model // tn
    nt = seq // lt
    m = sb * lt
    grid = (ns // sb, nt, nn)
    assert not emit or grid[:2] == (1, 1)
    w_names = ("w_co", "w_so", "w_ga", "w_gb")
    w_rows = (d_conf, d_conf, d_model, d_model)
    w_arrs, w_specs = zip(*[_col_tiled(w, name, rows, tn) for name, rows in zip(w_names, w_rows)])
    row3 = lambda s, t, n: (s, t, 0)
    fixed2 = lambda s, t, n: (0, 0)
    col2 = lambda s, t, n: (0, n)
    rowtile2 = lambda s, t, n: (n, 0)
    kern = functools.partial(_mix_out_kernel, sb=sb, lt=lt, nn=nn, emit=emit)
    res = pl.pallas_call(
        kern,
        grid=grid,
        in_specs=[
            pl.BlockSpec((sb, lt, d_model), row3),
            pl.BlockSpec((sb, lt, d_model), row3),
            pl.BlockSpec((sb, lt, d_conf), row3),
            pl.BlockSpec((sb, lt, d_conf), row3),
            *w_specs,
            pl.BlockSpec((tn, d_model), rowtile2),
            pl.BlockSpec((1, d_model), fixed2),
        ],
        out_specs=[pl.BlockSpec((sb, lt, d_model), row3)]
        + ([pl.BlockSpec((rows, tn), col2) for rows in w_rows]
           + [pl.BlockSpec((tn, d_model), rowtile2)] if emit else []),
        out_shape=[jax.ShapeDtypeStruct(x.shape, F32)]
        + ([jax.ShapeDtypeStruct((rows, d_model), BF16) for rows in w_rows]
           + [jax.ShapeDtypeStruct((d_model, d_model), BF16)] if emit else []),
        scratch_shapes=[
            pltpu.VMEM((m, d_model), F32),
        ],
        compiler_params=pltpu.CompilerParams(
            dimension_semantics=("arbitrary", "arbitrary", "arbitrary"),
            vmem_limit_bytes=VMEM_LIMIT_BYTES),
        name="mix_out",
    )(x, h, act_a, act_b, *w_arrs, w["w_o"], w["g_post_mix"])
    w_bf16 = {name: (arr, 0) for name, arr in zip(w_names, res[1:1 + len(w_names)])}
    if emit:
        w_bf16["w_o"] = res[-1]
    return res[0], w_bf16


def _ffn_kernel(x_ref, histg_ref, histv_ref, gpre_ref, wug_ref, wuv_ref, cwg_ref, cwv_ref,
                wd_ref, gpost_ref, out_ref, newg_ref, newv_ref, *rest, sb, lt, nj, wf, emit):
    if emit:
        wug_ref, wuv_ref, wd_ref = _bf16_weight_tiles((wug_ref, wuv_ref, wd_ref), rest[:3])
        rest = rest[3:]
    h_scr, acc_scr, carry_scr, xpg_scr, xpv_scr = rest
    t = pl.program_id(1)
    j = pl.program_id(2)
    m = sb * lt
    tj = wug_ref.shape[-1]
    pad = xpg_scr.shape[1] - lt

    @pl.when(j == 0)
    def _():
        _prenorm(h_scr, x_ref, gpre_ref, sb, lt)
        acc_scr[...] = jnp.zeros_like(acc_scr)

    @pl.when(t == 0)
    def _():
        for slot, hist_ref in ((j, histg_ref), (nj + j, histv_ref)):
            carry_scr[slot] = jnp.zeros(carry_scr.shape[1:], F32)
            carry_scr[slot, :, pad - (wf - 1):pad, :] = hist_ref[...]

    h = h_scr[...].reshape(m, -1)

    def conv(up, cw_ref, new_ref, slot, xp_scr):
        xp_scr[:, 0:pad, :] = carry_scr[slot]
        xp_scr[:, pad:pad + lt, :] = up.reshape(sb, lt, tj)
        y = None
        for k in range(wf):
            lo = pad - (wf - 1) + k
            term = cw_ref[k:k + 1, :] * xp_scr[:, lo:lo + lt, :]
            y = term if y is None else y + term
        carry_scr[slot] = xp_scr[:, lt:lt + pad, :]
        new_ref[...] = xp_scr[:, pad + lt - (wf - 1):pad + lt, :]
        return y.reshape(m, tj)

    f_gate = conv(_dot(h, wug_ref[...]), cwg_ref, newg_ref, j, xpg_scr)
    f_val = conv(_dot(h, wuv_ref[...]), cwv_ref, newv_ref, nj + j, xpv_scr)
    act = (f_gate * jax.nn.sigmoid(f_gate) * f_val).astype(BF16)
    acc_scr[...] += _dot(act, wd_ref[...])

    @pl.when(j == nj - 1)
    def _():
        _postnorm_residual(out_ref, x_ref, acc_scr, gpost_ref, sb, lt)


def _ffn(x, hist_f, w, *, sb, lt, tj, emit):
    ns, seq, d_model = x.shape
    d_ff = hist_f.shape[-1] // 2
    wf = hist_f.shape[1] + 1
    nj = d_ff // tj
    nt = seq // lt
    m = sb * lt
    pad = _round_up(wf - 1, SUBLANES)
    assert lt >= pad
    grid = (ns // sb, nt, nj)
    assert not emit or grid[:2] == (1, 1)
    (w_ug, spec_ug), (w_uv, spec_uv) = (_col_tiled(w, name, d_model, tj) for name in ("w_ug", "w_uv"))
    row3 = lambda s, t, j: (s, t, 0)
    fixed2 = lambda s, t, j: (0, 0)
    gate2 = lambda s, t, j: (0, j)
    val2 = lambda s, t, j: (0, nj + j)
    rowtile2 = lambda s, t, j: (j, 0)
    gate3 = lambda s, t, j: (s, 0, j)
    val3 = lambda s, t, j: (s, 0, nj + j)
    new3 = lambda s, t, j: (s, 0, jnp.where(t == nt - 1, j, 0))
    kern = functools.partial(_ffn_kernel, sb=sb, lt=lt, nj=nj, wf=wf, emit=emit)
    res = pl.pallas_call(
        kern,
        grid=grid,
        in_specs=[
            pl.BlockSpec((sb, lt, d_model), row3),
            pl.BlockSpec((sb, wf - 1, tj), gate3),
            pl.BlockSpec((sb, wf - 1, tj), val3),
            pl.BlockSpec((1, d_model), fixed2),
            spec_ug,
            spec_uv,
            pl.BlockSpec((wf, tj), gate2),
            pl.BlockSpec((wf, tj), val2),
            pl.BlockSpec((tj, d_model), rowtile2),
            pl.BlockSpec((1, d_model), fixed2),
        ],
        out_specs=[
            pl.BlockSpec((sb, lt, d_model), row3),
            pl.BlockSpec((sb, wf - 1, tj), new3),
            pl.BlockSpec((sb, wf - 1, tj), new3),
        ] + ([pl.BlockSpec((d_model, tj), gate2)] * 2 + [pl.BlockSpec((tj, d_model), rowtile2)]
             if emit else []),
        out_shape=[
            jax.ShapeDtypeStruct(x.shape, F32),
            jax.ShapeDtypeStruct((ns, wf - 1, d_ff), F32),
            jax.ShapeDtypeStruct((ns, wf - 1, d_ff), F32),
        ] + ([jax.ShapeDtypeStruct((d_model, d_ff), BF16)] * 2
             + [jax.ShapeDtypeStruct((d_ff, d_model), BF16)] if emit else []),
        scratch_shapes=[
            pltpu.VMEM((sb, lt, d_model), BF16),
            pltpu.VMEM((m, d_model), F32),
            pltpu.VMEM((2 * nj, sb, pad, tj), F32),
            pltpu.VMEM((sb, pad + lt, tj), F32),
            pltpu.VMEM((sb, pad + lt, tj), F32),
        ],
        compiler_params=pltpu.CompilerParams(
            dimension_semantics=("arbitrary", "arbitrary", "arbitrary"),
            vmem_limit_bytes=VMEM_LIMIT_BYTES),
        name="ffn",
    )(x, hist_f, hist_f, w["g_pre_ffn"], w_ug, w_uv, w["ffn_conv_w"], w["ffn_conv_w"],
      w["w_d"], w["g_post_ffn"])
    w_bf16 = {"w_ug": (res[3], 0), "w_uv": (res[4], 0), "w_d": res[5]} if emit else {}
    return res[0], jnp.concatenate([res[1], res[2]], axis=-1), w_bf16


def _layer(x, hist_a, hist_b, hist_f, w, *, sb, lt, tiles, lead_pad=0, emit=False):
    def zero_pad(v):
        return v.at[-1, :lead_pad].set(0.0) if lead_pad else v

    tc, tn, tj = tiles
    (act_a, act_b, new_a, new_b, h), w_in_bf16 = _mix_in(x, hist_a, hist_b, w, sb=sb, lt=lt, tc=tc, emit=emit)
    x1, w_out_bf16 = _mix_out(x, h, act_a, act_b, w, sb=sb, lt=lt, tn=tn, emit=emit)
    y, new_f, w_ffn_bf16 = _ffn(zero_pad(x1), hist_f, w, sb=sb, lt=lt, tj=tj, emit=emit)
    return zero_pad(y), new_a, new_b, new_f, {**w_in_bf16, **w_out_bf16, **w_ffn_bf16}


def kernel(x_prompt, x_sample, state_conf_conv, state_sconv, state_ffn_conv, meta_tokens, g_pre_mix, w_in, conf_conv_w, conf_conv_b, conf_ln_g, conf_ln_b, w_conf_out, sconv_w, w_sconv_out, w_o, g_post_mix, g_pre_ffn, w_up, ffn_conv_w, w_down, g_post_ffn):
    depth = w_in.shape[0]
    batch, seq, d_model = x_prompt.shape
    dec_batch, dec_seq, _ = x_sample.shape
    n_meta = meta_tokens.shape[0]
    d_conf, d_ff = w_conf_out.shape[1], w_down.shape[1]
    prompt_tile = 512
    assert seq % prompt_tile == 0

    assert n_meta <= dec_seq
    lead_pad = dec_seq - n_meta
    xm = jnp.pad(meta_tokens.astype(x_prompt.dtype), ((lead_pad, 0), (0, 0)))[None]
    xp = x_prompt
    xs = jnp.concatenate([x_sample, xm], axis=0)
    outs = [[] for _ in range(6)]
    for l in range(depth):
        small = {
            "g_pre_mix": g_pre_mix[l][None],
            "conf_conv_w": jnp.broadcast_to(conf_conv_w[l][:, None, :],
                                            (conf_conv_w.shape[1], SUBLANES, d_conf)),
            "conf_conv_b": conf_conv_b[l][None],
            "conf_ln_g": conf_ln_g[l][None], "conf_ln_b": conf_ln_b[l][None],
            "sconv_w": sconv_w[l], "g_post_mix": g_post_mix[l][None],
            "g_pre_ffn": g_pre_ffn[l][None], "ffn_conv_w": ffn_conv_w[l],
            "g_post_ffn": g_post_ffn[l][None],
        }
        w_f32 = {
            "w_val": (w_in[l], 0), "w_gate": (w_in[l], d_conf), "w_b": (w_in[l], 2 * d_conf),
            "w_c": (w_in[l], 3 * d_conf), "w_x": (w_in[l], 4 * d_conf),
            "w_ga": (w_in[l], 5 * d_conf), "w_gb": (w_in[l], 5 * d_conf + d_model),
            "w_co": (w_conf_out[l], 0), "w_so": (w_sconv_out[l], 0), "w_o": w_o[l],
            "w_ug": (w_up[l], 0), "w_uv": (w_up[l], d_ff), "w_d": w_down[l],
        }
        with_meta = lambda st: jnp.pad(st[l], ((0, 1), (0, 0), (0, 0)))
        xs, sa, sb_, sf, w_bf16 = _layer(
            xs, with_meta(state_conf_conv), with_meta(state_sconv), with_meta(state_ffn_conv),
            {**small, **w_f32}, sb=dec_batch + 1, lt=dec_seq, tiles=(128, 256, 256),
            lead_pad=lead_pad, emit=True)
        bcast = lambda hist: jnp.broadcast_to(hist[dec_batch:], (batch,) + hist.shape[1:])
        xp, pa, pb, pf, _ = _layer(xp, bcast(sa), bcast(sb_), bcast(sf), {**small, **w_bf16},
                                   sb=1, lt=prompt_tile, tiles=(512, 512, 512))
        for acc, val in zip(outs, (pa, pb, pf, sa[:dec_batch], sb_[:dec_batch], sf[:dec_batch])):
            acc.append(val)
    return (xp, xs[:dec_batch]) + tuple(jnp.stack(o) for o in outs)
```
